```python
import jax, jax.numpy as jnp
from jax import lax
import numpy as np

D_MODEL = 1024
BATCH = 8
SEQ = 4096
DEPTH = 1

N_MLA_HEADS = 8
MLA_Q_RANK = 256
MLA_KV_RANK = 128
MLA_NOPE_DIM = 64
MLA_ROPE_DIM = 32
MLA_V_DIM = 64
ROPE_THETA = 10000.0
Q_BLOCK = 128

N_GDN_HEADS = 8
GDN_HEAD_DIM = 64
GDN_CONV = 4
GDN_CHUNK = 64

MLA_W = N_MLA_HEADS * MLA_V_DIM
GDN_W = N_GDN_HEADS * GDN_HEAD_DIM
D_MIX = MLA_W + GDN_W
D_IN = MLA_Q_RANK + MLA_KV_RANK + MLA_ROPE_DIM + 3 * GDN_W + 2 * N_GDN_HEADS + GDN_W

D_FF = 2816
EPS = 1e-6

kernel_name = "hybrid_mla_gdn_macaron_sandwich"


def rmsnorm(x, g):
    xf = x.astype(jnp.float32)
    y = xf * lax.rsqrt(jnp.mean(xf * xf, axis=-1, keepdims=True) + EPS)
    return (y * g.astype(jnp.float32)).astype(x.dtype)


def l2norm(x):
    xf = x.astype(jnp.float32)
    return xf * lax.rsqrt(jnp.sum(xf * xf, axis=-1, keepdims=True) + EPS)


def swiglu(x, w_gate, w_up, w_down):
    return (jax.nn.silu(x @ w_gate) * (x @ w_up)) @ w_down


def rope_tables(positions):
    half = MLA_ROPE_DIM // 2
    freqs = ROPE_THETA ** (-jnp.arange(half, dtype=jnp.float32) / half)
    ang = positions.astype(jnp.float32)[..., None] * freqs
    return jnp.cos(ang), jnp.sin(ang)


def apply_rope(x, cos, sin):
    x1, x2 = jnp.split(x.astype(jnp.float32), 2, axis=-1)
    return jnp.concatenate([x1 * cos - x2 * sin, x1 * sin + x2 * cos], axis=-1).astype(x.dtype)


def mla_group(c_q, c_kv, k_rope_raw, positions, q_norm_g, w_uq, kv_norm_g, w_ukv):
    B, T, _ = c_q.shape
    H = N_MLA_HEADS
    q = (rmsnorm(c_q, q_norm_g) @ w_uq).reshape(B, T, H, MLA_NOPE_DIM + MLA_ROPE_DIM)
    q_nope, q_pe = q[..., :MLA_NOPE_DIM], q[..., MLA_NOPE_DIM:]
    kv = (rmsnorm(c_kv, kv_norm_g) @ w_ukv).reshape(B, T, H, MLA_NOPE_DIM + MLA_V_DIM)
    k_nope, v = kv[..., :MLA_NOPE_DIM], kv[..., MLA_NOPE_DIM:]
    cos, sin = rope_tables(positions)
    q_pe = apply_rope(q_pe, cos[:, :, None], sin[:, :, None])
    k_pe = apply_rope(k_rope_raw, cos, sin)
    scale = (MLA_NOPE_DIM + MLA_ROPE_DIM) ** -0.5
    nb = T // Q_BLOCK
    qn_b = q_nope.reshape(B, nb, Q_BLOCK, H, MLA_NOPE_DIM).transpose(1, 0, 2, 3, 4)
    qp_b = q_pe.reshape(B, nb, Q_BLOCK, H, MLA_ROPE_DIM).transpose(1, 0, 2, 3, 4)
    key_pos = jnp.arange(T)

    def attend(args):
        qn, qp, blk = args
        s = (jnp.einsum('bqhd,bkhd->bhqk', qn, k_nope)
             + jnp.einsum('bqhr,bkr->bhqk', qp, k_pe)).astype(jnp.float32) * scale
        q_pos = blk * Q_BLOCK + jnp.arange(Q_BLOCK)
        causal = key_pos[None, :] <= q_pos[:, None]
        s = jnp.where(causal, s, -jnp.inf)
        p = jax.nn.softmax(s, axis=-1).astype(v.dtype)
        return jnp.einsum('bhqk,bkhd->bqhd', p, v)

    o = lax.map(attend, (qn_b, qp_b, jnp.arange(nb)))
    return o.transpose(1, 0, 2, 3, 4).reshape(B, T, H * MLA_V_DIM)


def causal_conv(x, w):
    K, C = w.shape
    return lax.conv_general_dilated(
        x, w[:, None, :], window_strides=(1,), padding=[(K - 1, 0)],
        dimension_numbers=('NWC', 'WIO', 'NWC'), feature_group_count=C)


def gated_delta_rule(q, k, v, g, beta):
    out_dtype = v.dtype
    B, T, H, dk = q.shape
    dv = v.shape[-1]
    C = GDN_CHUNK
    N = T // C
    f32 = jnp.float32
    q = q.astype(f32) * dk ** -0.5
    k, v, g, beta = k.astype(f32), v.astype(f32), g.astype(f32), beta.astype(f32)

    def to_chunks(t):
        return t.reshape((B, N, C, H) + t.shape[3:]).swapaxes(2, 3)

    qc, kc, vc, gc, bc = map(to_chunks, (q, k, v, g, beta))
    gc = jnp.cumsum(gc, axis=-1)
    tril = jnp.tril(jnp.ones((C, C), dtype=bool))
    strict = jnp.tril(jnp.ones((C, C), dtype=bool), -1)
    diff = gc[..., :, None] - gc[..., None, :]
    decay = jnp.exp(jnp.where(tril, diff, -jnp.inf))
    kb = kc * bc[..., None]
    L = jnp.where(strict, jnp.einsum('bnhid,bnhjd->bnhij', kb, kc) * decay, 0.0)
    A = jnp.eye(C, dtype=f32) + L
    w = lax.linalg.triangular_solve(A, kb * jnp.exp(gc)[..., None],
                                    left_side=True, lower=True, unit_diagonal=True)
    u = lax.linalg.triangular_solve(A, vc * bc[..., None],
                                    left_side=True, lower=True, unit_diagonal=True)
    attn = jnp.einsum('bnhid,bnhjd->bnhij', qc, kc) * decay
    q_dec = qc * jnp.exp(gc)[..., None]
    k_dec = kc * jnp.exp(gc[..., -1:] - gc)[..., None]
    g_last = jnp.exp(gc[..., -1])

    def step(S, xs):
        q_d, w_i, u_i, attn_i, k_d, gl = xs
        v_new = u_i - jnp.einsum('bhcd,bhde->bhce', w_i, S)
        o = jnp.einsum('bhcd,bhde->bhce', q_d, S) + jnp.einsum('bhij,bhje->bhie', attn_i, v_new)
        S = S * gl[..., None, None] + jnp.einsum('bhcd,bhce->bhde', k_d, v_new)
        return S, o

    xs = tuple(jnp.moveaxis(t, 1, 0) for t in (q_dec, w, u, attn, k_dec, g_last))
    S0 = jnp.zeros((B, H, dk, dv), f32)
    _, o = lax.scan(step, S0, xs)
    return o.transpose(1, 0, 3, 2, 4).reshape(B, T, H, dv).astype(out_dtype)


def setup_inputs(seed: int = 0) -> dict:
    key = jax.random.key(seed)
    ks = jax.random.split(key, 32)
    f32 = jnp.float32

    def nrm(k, shape, fan_in):
        return jax.random.normal(k, shape, f32) * fan_in ** -0.5

    def gain(k, shape):
        return 1.0 + 0.02 * jax.random.normal(k, shape, f32)

    L = DEPTH
    x = jax.random.normal(ks[0], (BATCH, SEQ, D_MODEL), f32)
    offset = jax.random.randint(ks[1], (BATCH, 1), 0, 2048, dtype=jnp.int32)
    positions = offset + jnp.arange(SEQ, dtype=jnp.int32)[None, :]
    a_log = jnp.log(jax.random.uniform(ks[17], (L, N_GDN_HEADS), f32, 1.0, 16.0))
    dt = jnp.exp(jax.random.uniform(ks[18], (L, N_GDN_HEADS), f32, np.log(1e-3), np.log(1e-1)))
    dt_bias = dt + jnp.log(-jnp.expm1(-dt))
    return {
        "x": x,
        "positions": positions,
        "ffn1_pre_g": gain(ks[2], (L, D_MODEL)),
        "ffn1_w_gate": nrm(ks[3], (L, D_MODEL, D_FF), D_MODEL),
        "ffn1_w_up": nrm(ks[4], (L, D_MODEL, D_FF), D_MODEL),
        "ffn1_w_down": nrm(ks[5], (L, D_FF, D_MODEL), D_FF),
        "ffn1_post_g": gain(ks[6], (L, D_MODEL)),
        "mix_pre_g": gain(ks[7], (L, D_MODEL)),
        "w_in": nrm(ks[8], (L, D_MODEL, D_IN), D_MODEL),
        "mla_q_norm_g": gain(ks[9], (L, MLA_Q_RANK)),
        "mla_w_uq": nrm(ks[10], (L, MLA_Q_RANK, N_MLA_HEADS * (MLA_NOPE_DIM + MLA_ROPE_DIM)), MLA_Q_RANK),
        "mla_kv_norm_g": gain(ks[11], (L, MLA_KV_RANK)),
        "mla_w_ukv": nrm(ks[12], (L, MLA_KV_RANK, N_MLA_HEADS * (MLA_NOPE_DIM + MLA_V_DIM)), MLA_KV_RANK),
        "mla_out_g": gain(ks[13], (L, MLA_W)),
        "gdn_conv_w": nrm(ks[14], (L, GDN_CONV, 3 * GDN_W), GDN_CONV),
        "gdn_a_log": a_log,
        "gdn_dt_bias": dt_bias,
        "gdn_norm_g": gain(ks[15], (L, GDN_HEAD_DIM)),
        "w_out": nrm(ks[16], (L, D_MIX, D_MODEL), D_MIX),
        "mix_post_g": gain(ks[19], (L, D_MODEL)),
        "ffn2_pre_g": gain(ks[20], (L, D_MODEL)),
        "ffn2_w_gate": nrm(ks[21], (L, D_MODEL, D_FF), D_MODEL),
        "ffn2_w_up": nrm(ks[22], (L, D_MODEL, D_FF), D_MODEL),
        "ffn2_w_down": nrm(ks[23], (L, D_FF, D_MODEL), D_FF),
        "ffn2_post_g": gain(ks[24], (L, D_MODEL)),
    }


def reference(x, positions, ffn1_pre_g, ffn1_w_gate, ffn1_w_up, ffn1_w_down, ffn1_post_g,
              mix_pre_g, w_in, mla_q_norm_g, mla_w_uq, mla_kv_norm_g, mla_w_ukv, mla_out_g,
              gdn_conv_w, gdn_a_log, gdn_dt_bias, gdn_norm_g, w_out, mix_post_g,
              ffn2_pre_g, ffn2_w_gate, ffn2_w_up, ffn2_w_down, ffn2_post_g):
    B, T, _ = x.shape
    H, dh = N_GDN_HEADS, GDN_HEAD_DIM
    sizes = (MLA_Q_RANK, MLA_KV_RANK, MLA_ROPE_DIM, 3 * GDN_W, N_GDN_HEADS, N_GDN_HEADS, GDN_W)
    cuts = []
    acc = 0
    for s in sizes[:-1]:
        acc += s
        cuts.append(acc)

    for l in range(DEPTH):
        h = swiglu(rmsnorm(x, ffn1_pre_g[l]), ffn1_w_gate[l], ffn1_w_up[l], ffn1_w_down[l])
        x = x + 0.5 * rmsnorm(h, ffn1_post_g[l])

        hn = rmsnorm(x, mix_pre_g[l])
        proj = hn @ w_in[l]
        c_q, c_kv, k_pe_raw, qkv, a, b, gate = jnp.split(proj, cuts, axis=-1)

        mla_o = mla_group(c_q, c_kv, k_pe_raw, positions,
                          mla_q_norm_g[l], mla_w_uq[l], mla_kv_norm_g[l], mla_w_ukv[l])
        mla_o = rmsnorm(mla_o, mla_out_g[l])

        qkv = jax.nn.silu(causal_conv(qkv, gdn_conv_w[l]))
        q, k, v = jnp.split(qkv, 3, axis=-1)
        q = l2norm(q.reshape(B, T, H, dh))
        k = l2norm(k.reshape(B, T, H, dh))
        v = v.reshape(B, T, H, dh)
        g = -jnp.exp(gdn_a_log[l].astype(jnp.float32)) * jax.nn.softplus(
            a.astype(jnp.float32) + gdn_dt_bias[l].astype(jnp.float32))
        beta = jax.nn.sigmoid(b.astype(jnp.float32))
        o = gated_delta_rule(q, k, v, g, beta)
        o = rmsnorm(o, gdn_norm_g[l]) * jax.nn.silu(gate.reshape(B, T, H, dh))
        gdn_o = o.reshape(B, T, GDN_W)

        mixed = jnp.concatenate([mla_o, gdn_o], axis=-1) @ w_out[l]
        x = x + rmsnorm(mixed, mix_post_g[l])

        h = swiglu(rmsnorm(x, ffn2_pre_g[l]), ffn2_w_gate[l], ffn2_w_up[l], ffn2_w_down[l])
        x = x + 0.5 * rmsnorm(h, ffn2_post_g[l])
    return x
```

```python
import functools

import jax
import jax.numpy as jnp
import numpy as np
from jax import lax
from jax.experimental import pallas as pl
from jax.experimental.pallas import tpu as pltpu

F32 = jnp.float32
BF16 = jnp.bfloat16

D_MODEL = 1024
N_MLA_HEADS = 8
MLA_Q_RANK = 256
MLA_KV_RANK = 128
MLA_NOPE_DIM = 64
MLA_ROPE_DIM = 32
MLA_V_DIM = 64
ROPE_THETA = 10000.0
N_GDN_HEADS = 8
GDN_HEAD_DIM = 64
GDN_CONV = 4
GDN_CHUNK = 64
MLA_W = N_MLA_HEADS * MLA_V_DIM
GDN_W = N_GDN_HEADS * GDN_HEAD_DIM
D_FF = 2816
EPS = 1e-6

LANES = 128
HEAD_PAD = 128
NEG_BIG = -1e30
VMEM_LIMIT = 56 * 1024 * 1024

_PROJ_BLOCKS = (("cq", MLA_Q_RANK), ("ckv", MLA_KV_RANK), ("kpe", LANES), ("kper", LANES),
                ("qkv", 3 * GDN_W), ("gate", GDN_W), ("a", GDN_W), ("b", GDN_W))
_PROJ_OFF = {}
_off = 0
for _name, _width in _PROJ_BLOCKS:
    _PROJ_OFF[_name] = (_off, _off + _width)
    _off += _width
PROJ_COLS = _off


def _rms(x, g):
    return x * lax.rsqrt(jnp.mean(x * x, axis=-1, keepdims=True) + EPS) * g


def _dot(a, b):
    return jnp.dot(a, b, preferred_element_type=F32)


def _dot_hi(a, b):
    return jnp.dot(a, b, preferred_element_type=F32, precision=lax.Precision.HIGHEST)


def _const_spec(shape):
    nd = len(shape)
    return pl.BlockSpec(shape, lambda *_: (0,) * nd, pipeline_mode=pl.Buffered(1))


def _swiglu_residual(x, pre_g, wg_ref, wu_ref, wd_ref, post_g, h_ref, fc):
    xn = _rms(x, pre_g).astype(BF16)
    for c in range(wg_ref.shape[1] // fc):
        sl = slice(c * fc, (c + 1) * fc)
        g = _dot(xn, wg_ref[:, sl])
        u = _dot(xn, wu_ref[:, sl])
        h_ref[:, sl] = (g * jax.nn.sigmoid(g) * u).astype(BF16)
    y = _dot(h_ref[...], wd_ref[...])
    return x + 0.5 * _rms(y, post_g)


def _ffn_kernel(x_ref, pre_g_ref, wg_ref, wu_ref, wd_ref, post_g_ref, o_ref, h_ref, *, fc):
    o_ref[...] = _swiglu_residual(x_ref[...], pre_g_ref[...], wg_ref, wu_ref, wd_ref,
                                  post_g_ref[...], h_ref, fc)


def _ffn_mix_kernel(x_ref, mla_ref, gdn_ref, mla_g_ref, wo_ref, mix_g_ref,
                    pre_g_ref, wg_ref, wu_ref, wd_ref, post_g_ref, o_ref, h_ref, *, fc):
    mla = _rms(mla_ref[...], mla_g_ref[...]).astype(BF16)
    mixed = _dot(mla, wo_ref[:MLA_W, :]) + _dot(gdn_ref[...].astype(BF16), wo_ref[MLA_W:, :])
    x2 = x_ref[...] + _rms(mixed, mix_g_ref[...])
    o_ref[...] = _swiglu_residual(x2, pre_g_ref[...], wg_ref, wu_ref, wd_ref,
                                  post_g_ref[...], h_ref, fc)


def _ffn_call(x, pre_g, wg, wu, wd, post_g, mix=None, *, tm=512, fc=256):
    n, d = x.shape
    f = wg.shape[1]
    row = lambda w: pl.BlockSpec((tm, w), lambda i: (i, 0))
    ffn_specs = [_const_spec((1, d)), _const_spec((d, f)), _const_spec((d, f)),
                 _const_spec((f, d)), _const_spec((1, d))]
    ffn_args = [pre_g, wg, wu, wd, post_g]
    if mix is None:
        body, in_specs, args = _ffn_kernel, [row(d)] + ffn_specs, [x] + ffn_args
    else:
        mla_o, gdn_o, mla_g, wo, mix_g = mix
        body = _ffn_mix_kernel
        in_specs = [row(d), row(MLA_W), row(GDN_W), _const_spec((1, MLA_W)),
                    _const_spec(wo.shape), _const_spec((1, d))] + ffn_specs
        args = [x, mla_o, gdn_o, mla_g, wo, mix_g] + ffn_args
    return pl.pallas_call(
        functools.partial(body, fc=fc),
        grid=(n // tm,),
        in_specs=in_specs,
        out_specs=row(d),
        out_shape=jax.ShapeDtypeStruct((n, d), F32),
        scratch_shapes=[pltpu.VMEM((tm, f), BF16)],
        compiler_params=pltpu.CompilerParams(dimension_semantics=("arbitrary",),
                                             vmem_limit_bytes=VMEM_LIMIT),
        name="ffn_mix" if mix is not None else "ffn",
    )(*args)


def _proj_kernel(x_ref, pos_ref, g_ref, w_ref, qg_ref, wq1_ref, wq2_ref, kvg_ref, wk_ref, wv_ref,
                 freq_ref, q_ref, k_ref, v_ref, qkv_ref, gate_ref, a_ref, b_ref):
    hn = _rms(x_ref[...], g_ref[...]).astype(BF16)

    def blk(name):
        lo, hi = _PROJ_OFF[name]
        return _dot(hn, w_ref[:, lo:hi])

    ang = pos_ref[...].astype(F32) * freq_ref[...]
    cos, sin = jnp.cos(ang), jnp.sin(ang)
    scale = (MLA_NOPE_DIM + MLA_ROPE_DIM) ** -0.5

    cq = _rms(blk("cq"), qg_ref[...]).astype(BF16)
    reps = (1, N_MLA_HEADS)
    q = _dot(cq, wq1_ref[...]) * jnp.tile(cos * scale, reps) + _dot(cq, wq2_ref[...]) * jnp.tile(sin * scale, reps)
    q_ref[...] = q.astype(BF16)

    ckv = _rms(blk("ckv"), kvg_ref[...]).astype(BF16)
    kpe = blk("kpe") * cos + blk("kper") * sin
    k_ref[...] = (_dot(ckv, wk_ref[...]) + jnp.tile(kpe, reps)).astype(BF16)
    v_ref[...] = _dot(ckv, wv_ref[...]).astype(BF16)

    qkv_ref[...] = blk("qkv")
    gate_ref[...] = blk("gate")
    a_ref[...] = blk("a")
    b_ref[...] = blk("b")


def _proj_call(x1, pos, g, w_all, qg, wq1, wq2, kvg, wk, wv, freq, *, tm=512):
    n, d = x1.shape
    row = lambda w: pl.BlockSpec((tm, w), lambda i: (i, 0))
    hw = N_MLA_HEADS * HEAD_PAD
    outs = [(hw, BF16), (hw, BF16), (MLA_W, BF16), (3 * GDN_W, F32), (GDN_W, F32), (GDN_W, F32), (GDN_W, F32)]
    consts = [g, w_all, qg, wq1, wq2, kvg, wk, wv, freq]
    return pl.pallas_call(
        _proj_kernel,
        grid=(n // tm,),
        in_specs=[row(d), row(1)] + [_const_spec(c.shape) for c in consts],
        out_specs=[row(w) for w, _ in outs],
        out_shape=[jax.ShapeDtypeStruct((n, w), dt) for w, dt in outs],
        compiler_params=pltpu.CompilerParams(dimension_semantics=("arbitrary",),
                                             vmem_limit_bytes=VMEM_LIMIT),
        name="proj",
    )(x1, pos, *consts)


def _attn_kernel(q_ref, k_ref, v_ref, o_ref, m_ref, l_ref, acc_ref, *, blk):
    qi, ki = pl.program_id(2), pl.program_id(3)

    @pl.when(ki == 0)
    def _():
        m_ref[...] = jnp.full(m_ref.shape, NEG_BIG, F32)
        l_ref[...] = jnp.zeros(l_ref.shape, F32)
        acc_ref[...] = jnp.zeros(acc_ref.shape, F32)

    def step(masked):
        v = v_ref[...]
        for hh in range(2):
            sl = slice(hh * HEAD_PAD, (hh + 1) * HEAD_PAD)
            s = lax.dot_general(q_ref[:, sl], k_ref[:, sl], (((1,), (1,)), ((), ())),
                                preferred_element_type=F32)
            if masked:
                row = lax.broadcasted_iota(jnp.int32, s.shape, 0)
                col = lax.broadcasted_iota(jnp.int32, s.shape, 1)
                s = jnp.where(col <= row, s, NEG_BIG)
            m_old = m_ref[hh]
            m_new = jnp.maximum(m_old, jnp.max(s, axis=-1, keepdims=True))
            alpha = jnp.exp(m_old - m_new)
            p = jnp.exp(s - m_new)
            l_ref[hh] = alpha * l_ref[hh] + jnp.sum(p, axis=-1, keepdims=True)
            acc_ref[hh] = alpha * acc_ref[hh] + _dot(p.astype(BF16), v)
            m_ref[hh] = m_new

    @pl.when(ki < qi)
    def _():
        step(False)

    @pl.when(ki == qi)
    def _():
        step(True)
        lane = lax.broadcasted_iota(jnp.int32, (blk, LANES), 1)
        o_ref[...] = jnp.where(lane < MLA_V_DIM, acc_ref[0] / l_ref[0], acc_ref[1] / l_ref[1])


def _attn_call(q, k, v, *, batch, seq, blk=512):
    nb = seq // blk
    pairs = N_MLA_HEADS // 2
    q_spec = pl.BlockSpec((blk, 2 * HEAD_PAD), lambda b, p, qi, ki: (b * nb + qi, p))
    k_spec = pl.BlockSpec((blk, 2 * HEAD_PAD), lambda b, p, qi, ki: (b * nb + jnp.minimum(ki, qi), p))
    v_spec = pl.BlockSpec((blk, LANES), lambda b, p, qi, ki: (b * nb + jnp.minimum(ki, qi), p))
    o_spec = pl.BlockSpec((blk, LANES), lambda b, p, qi, ki: (b * nb + qi, p))
    return pl.pallas_call(
        functools.partial(_attn_kernel, blk=blk),
        grid=(batch, pairs, nb, nb),
        in_specs=[q_spec, k_spec, v_spec],
        out_specs=o_spec,
        out_shape=jax.ShapeDtypeStruct((batch * seq, MLA_W), F32),
        scratch_shapes=[pltpu.VMEM((2, blk, 1), F32), pltpu.VMEM((2, blk, 1), F32),
                        pltpu.VMEM((2, blk, LANES), F32)],
        compiler_params=pltpu.CompilerParams(
            dimension_semantics=("arbitrary", "arbitrary", "arbitrary", "arbitrary"),
            vmem_limit_bytes=VMEM_LIMIT),
        name="attn",
    )(q, k, v)


def _lo_mask(shape):
    return lax.broadcasted_iota(jnp.int32, shape, 1) % LANES < GDN_HEAD_DIM


def _pair_blockdiag(x_pair):
    lo = _lo_mask(x_pair.shape)
    zero = jnp.zeros_like(x_pair)
    return jnp.concatenate([jnp.where(lo, x_pair, zero), jnp.where(lo, zero, x_pair)], axis=0)


def _heads_matmul(a_all, b_all, dot):
    outs = []
    for p in range(N_GDN_HEADS // 2):
        sl = slice(p * LANES, (p + 1) * LANES)
        outs.append(dot(a_all[:, sl], _pair_blockdiag(b_all[:, sl])))
    return jnp.concatenate(outs, axis=1)


def _heads_gram(a_all, b_all):
    outs = []
    for p in range(N_GDN_HEADS // 2):
        sl = slice(p * LANES, (p + 1) * LANES)
        outs.append(lax.dot_general(a_all[:, sl].astype(BF16), _pair_blockdiag(b_all[:, sl]).astype(BF16),
                                    (((1,), (1,)), ((), ())), preferred_element_type=F32))
    return jnp.concatenate(outs, axis=1)


def _heads_outer(a_all, b_all):
    outs = []
    for p in range(N_GDN_HEADS // 2):
        sl = slice(p * LANES, (p + 1) * LANES)
        full = lax.dot_general(a_all[:, sl].astype(BF16), b_all[:, sl].astype(BF16),
                               (((0,), (0,)), ((), ())), preferred_element_type=F32)
        outs.append(jnp.where(_lo_mask((GDN_HEAD_DIM, LANES)), full[:GDN_HEAD_DIM], full[GDN_HEAD_DIM:]))
    return jnp.concatenate(outs, axis=1)


def _unit_lower_inverse(l_all, row, col):
    eye = jnp.where(row == col, 1.0, 0.0).astype(F32)
    zero = jnp.zeros_like(l_all)
    same16 = (row // 16) == (col // 16)
    same32 = (row // 32) == (col // 32)
    mm = functools.partial(_heads_matmul, dot=_dot_hi)
    m1 = jnp.where(same16, -l_all, zero)
    inv = eye + m1
    power = m1
    for _ in range(3):
        power = mm(power, power)
        inv = inv + mm(power, inv)
    for off in (jnp.where(same32 & ~same16, l_all, zero), jnp.where(same32, zero, l_all)):
        inv = inv - mm(inv, mm(off, inv))
    return inv


def _gdn_kernel(qkv_ref, gate_ref, a_ref, b_ref, convw_ref, alog_ref, dtb_ref, ng_ref, ones_ref,
                o_ref, xbuf_ref, s_ref, *, tm):
    c = GDN_CHUNK
    w = GDN_W
    halo = 8

    @pl.when(pl.program_id(1) == 0)
    def _():
        xbuf_ref[0:halo, :] = jnp.zeros((halo, 3 * w), F32)
        s_ref[...] = jnp.zeros(s_ref.shape, F32)

    xbuf_ref[halo:halo + tm, :] = qkv_ref[...]
    y = jnp.zeros((tm, 3 * w), F32)
    for kk in range(GDN_CONV):
        start = halo - (GDN_CONV - 1) + kk
        y = y + xbuf_ref[start:start + tm, :] * convw_ref[kk:kk + 1, :]
    xbuf_ref[0:halo, :] = xbuf_ref[tm:tm + halo, :]
    y = y * jax.nn.sigmoid(y)

    ones_bd = ones_ref[...]

    def head_sumsq(x):
        x2 = x * x
        hi = x2.astype(BF16)
        lo = (x2 - hi.astype(F32)).astype(BF16)
        return _dot(hi, ones_bd) + _dot(lo, ones_bd)

    q = y[:, 0:w]
    k = y[:, w:2 * w]
    v = y[:, 2 * w:3 * w]
    q = q * lax.rsqrt(head_sumsq(q) + EPS) * (GDN_HEAD_DIM ** -0.5)
    k = k * lax.rsqrt(head_sumsq(k) + EPS)

    sp_in = a_ref[...] + dtb_ref[...]
    softplus = jnp.maximum(sp_in, 0.0) + jnp.log1p(jnp.exp(-jnp.abs(sp_in)))
    g = -jnp.exp(alog_ref[...]) * softplus
    beta = jax.nn.sigmoid(b_ref[...])
    kb = k * beta
    vb = v * beta

    row = lax.broadcasted_iota(jnp.int32, (c, w), 0)
    col = lax.broadcasted_iota(jnp.int32, (c, w), 1) % GDN_HEAD_DIM
    tril = jnp.where(lax.broadcasted_iota(jnp.int32, (c, c), 1) <= lax.broadcasted_iota(jnp.int32, (c, c), 0),
                     1.0, 0.0).astype(F32)
    ones_cc = jnp.ones((c, c), F32)
    zero = jnp.zeros((c, w), F32)

    outs = []
    for ci in range(tm // c):
        rs = slice(ci * c, (ci + 1) * c)
        gc = _dot_hi(tril, g[rs])
        gc_row = _dot_hi(ones_cc, jnp.where(row == col, gc, zero))
        decay = jnp.exp(jnp.where(col <= row, gc - gc_row, NEG_BIG))
        egc = jnp.exp(gc)
        gc_last = gc[c - 1:c, :]
        kc, qc = k[rs], q[rs]

        gram = _heads_gram(jnp.concatenate([kb[rs], qc], axis=0), kc)
        l_all = jnp.where(col < row, gram[:c] * decay, zero)
        attn = gram[c:] * decay
        t_all = _unit_lower_inverse(l_all, row, col)
        mm_hi = functools.partial(_heads_matmul, dot=_dot_hi)
        w_all = mm_hi(t_all, kb[rs] * egc)
        u_all = mm_hi(t_all, vb[rs])

        s_all = s_ref[...]
        mm_bf = functools.partial(_heads_matmul, dot=lambda a, b: _dot(a.astype(BF16), b.astype(BF16)))
        ws_qs = mm_bf(jnp.concatenate([w_all, qc * egc], axis=0), s_all)
        v_new = u_all - ws_qs[:c]
        outs.append(ws_qs[c:] + mm_bf(attn, v_new))
        k_dec = kc * jnp.exp(gc_last - gc)
        s_ref[...] = s_all * jnp.exp(gc_last) + _heads_outer(k_dec, v_new)

    o = jnp.concatenate(outs, axis=0)
    var = head_sumsq(o) * (1.0 / GDN_HEAD_DIM)
    gate = gate_ref[...]
    o_ref[...] = o * lax.rsqrt(var + EPS) * ng_ref[...] * (gate * jax.nn.sigmoid(gate))


def _gdn_call(qkv, gate, a_e, b_e, convw, alog_e, dtb_e, ng_e, ones_bd, *, batch, seq, tm=256):
    nt = seq // tm
    row = lambda wd: pl.BlockSpec((tm, wd), lambda b, j: (b * nt + j, 0))
    consts = [convw, alog_e, dtb_e, ng_e, ones_bd]
    return pl.pallas_call(
        functools.partial(_gdn_kernel, tm=tm),
        grid=(batch, nt),
        in_specs=[row(3 * GDN_W), row(GDN_W), row(GDN_W), row(GDN_W)] + [_const_spec(x.shape) for x in consts],
        out_specs=row(GDN_W),
        out_shape=jax.ShapeDtypeStruct((batch * seq, GDN_W), F32),
        scratch_shapes=[pltpu.VMEM((tm + 8, 3 * GDN_W), F32), pltpu.VMEM((GDN_HEAD_DIM, GDN_W), F32)],
        compiler_params=pltpu.CompilerParams(dimension_semantics=("arbitrary", "arbitrary"),
                                             vmem_limit_bytes=VMEM_LIMIT),
        name="gdn",
    )(qkv, gate, a_e, b_e, *consts)


def _rotate_half_cols(w):
    half = MLA_ROPE_DIM // 2
    return jnp.concatenate([-w[:, half:], w[:, :half]], axis=1)


def _prep_proj_weights(w_in, w_uq, w_ukv):
    d = w_in.shape[0]
    sizes = (MLA_Q_RANK, MLA_KV_RANK, MLA_ROPE_DIM, 3 * GDN_W, N_GDN_HEADS, N_GDN_HEADS, GDN_W)
    cuts = np.cumsum(sizes)[:-1]
    w_cq, w_ckv, w_kpe, w_qkv, w_a, w_b, w_gate = jnp.split(w_in, cuts, axis=1)

    def rope_block(wpe):
        z_lo = jnp.zeros((d, MLA_NOPE_DIM), F32)
        z_hi = jnp.zeros((d, HEAD_PAD - MLA_NOPE_DIM - MLA_ROPE_DIM), F32)
        return jnp.concatenate([z_lo, wpe, z_hi], axis=1)

    w_all = jnp.concatenate([
        w_cq, w_ckv, rope_block(w_kpe), rope_block(_rotate_half_cols(w_kpe)), w_qkv, w_gate,
        jnp.repeat(w_a, GDN_HEAD_DIM, axis=1), jnp.repeat(w_b, GDN_HEAD_DIM, axis=1)], axis=1).astype(BF16)

    qh = w_uq.reshape(MLA_Q_RANK, N_MLA_HEADS, MLA_NOPE_DIM + MLA_ROPE_DIM)
    q_nope, q_pe = qh[..., :MLA_NOPE_DIM], qh[..., MLA_NOPE_DIM:]
    q_pe_rot = jnp.concatenate([-q_pe[..., MLA_ROPE_DIM // 2:], q_pe[..., :MLA_ROPE_DIM // 2]], axis=-1)
    pad = jnp.zeros((MLA_Q_RANK, N_MLA_HEADS, HEAD_PAD - MLA_NOPE_DIM - MLA_ROPE_DIM), F32)
    wq1 = jnp.concatenate([q_nope, q_pe, pad], axis=-1).reshape(MLA_Q_RANK, -1).astype(BF16)
    wq2 = jnp.concatenate([jnp.zeros_like(q_nope), q_pe_rot, pad], axis=-1).reshape(MLA_Q_RANK, -1).astype(BF16)

    kvh = w_ukv.reshape(MLA_KV_RANK, N_MLA_HEADS, MLA_NOPE_DIM + MLA_V_DIM)
    k_nope, v = kvh[..., :MLA_NOPE_DIM], kvh[..., MLA_NOPE_DIM:]
    kpad = jnp.zeros((MLA_KV_RANK, N_MLA_HEADS, HEAD_PAD - MLA_NOPE_DIM), F32)
    wk = jnp.concatenate([k_nope, kpad], axis=-1).reshape(MLA_KV_RANK, -1).astype(BF16)
    wv = v.reshape(MLA_KV_RANK, -1).astype(BF16)
    return w_all, wq1, wq2, wk, wv


def _rope_lane_freqs():
    half = MLA_ROPE_DIM // 2
    freqs = ROPE_THETA ** (-jnp.arange(half, dtype=F32) / half)
    return jnp.concatenate([jnp.zeros((MLA_NOPE_DIM,), F32), freqs, freqs,
                            jnp.zeros((HEAD_PAD - MLA_NOPE_DIM - MLA_ROPE_DIM,), F32)])[None, :]


def kernel(x, positions, ffn1_pre_g, ffn1_w_gate, ffn1_w_up, ffn1_w_down, ffn1_post_g, mix_pre_g, w_in, mla_q_norm_g, mla_w_uq, mla_kv_norm_g, mla_w_ukv, mla_out_g, gdn_conv_w, gdn_a_log, gdn_dt_bias, gdn_norm_g, w_out, mix_post_g, ffn2_pre_g, ffn2_w_gate, ffn2_w_up, ffn2_w_down, ffn2_post_g):
    batch, seq, d = x.shape
    n = batch * seq
    xt = x.reshape(n, d)
    pos = positions.reshape(n, 1)
    freq = _rope_lane_freqs()
    head_ids = jnp.arange(GDN_W) // GDN_HEAD_DIM
    ones_bd = (head_ids[:, None] == head_ids[None, :]).astype(BF16)
    bf = lambda w: w.astype(BF16)
    r = lambda g: g[None, :]

    for l in range(ffn1_pre_g.shape[0]):
        xt = _ffn_call(xt, r(ffn1_pre_g[l]), bf(ffn1_w_gate[l]), bf(ffn1_w_up[l]), bf(ffn1_w_down[l]),
                       r(ffn1_post_g[l]))
        w_all, wq1, wq2, wk, wv = _prep_proj_weights(w_in[l], mla_w_uq[l], mla_w_ukv[l])
        q, k, v, qkv, gate, a_e, b_e = _proj_call(xt, pos, r(mix_pre_g[l]), w_all, r(mla_q_norm_g[l]), wq1, wq2,
                                                  r(mla_kv_norm_g[l]), wk, wv, freq)
        mla_o = _attn_call(q, k, v, batch=batch, seq=seq)
        gdn_o = _gdn_call(qkv, gate, a_e, b_e, gdn_conv_w[l],
                          r(jnp.repeat(gdn_a_log[l].astype(F32), GDN_HEAD_DIM)),
                          r(jnp.repeat(gdn_dt_bias[l].astype(F32), GDN_HEAD_DIM)),
                          r(jnp.tile(gdn_norm_g[l], N_GDN_HEADS)), ones_bd, batch=batch, seq=seq)
        xt = _ffn_call(xt, r(ffn2_pre_g[l]), bf(ffn2_w_gate[l]), bf(ffn2_w_up[l]), bf(ffn2_w_down[l]),
                       r(ffn2_post_g[l]),
                       mix=(mla_o, gdn_o, r(mla_out_g[l]), bf(w_out[l]), r(mix_post_g[l])))
    return xt.reshape(batch, seq, d)
```

```python
import functools

import jax
import jax.numpy as jnp
import numpy as np
from jax import lax
from jax.experimental import pallas as pl
from jax.experimental.pallas import tpu as pltpu

F32 = jnp.float32
BF16 = jnp.bfloat16

D_MODEL = 1024
N_MLA_HEADS = 8
MLA_Q_RANK = 256
MLA_KV_RANK = 128
MLA_NOPE_DIM = 64
MLA_ROPE_DIM = 32
MLA_V_DIM = 64
ROPE_THETA = 10000.0
N_GDN_HEADS = 8
GDN_HEAD_DIM = 64
GDN_CONV = 4
GDN_CHUNK = 64
MLA_W = N_MLA_HEADS * MLA_V_DIM
GDN_W = N_GDN_HEADS * GDN_HEAD_DIM
D_FF = 2816
EPS = 1e-6

LANES = 128
HEAD_PAD = 128
NEG_BIG = -1e30
LOG2_E = 1.4426950408889634
VMEM_LIMIT = 56 * 1024 * 1024

_PROJ_BLOCKS = (("cq", MLA_Q_RANK), ("ckv", MLA_KV_RANK), ("kpe", LANES), ("kper", LANES),
                ("qkv", 3 * GDN_W), ("gate", GDN_W), ("a", GDN_W), ("b", GDN_W))
_PROJ_OFF = {}
_off = 0
for _name, _width in _PROJ_BLOCKS:
    _PROJ_OFF[_name] = (_off, _off + _width)
    _off += _width
PROJ_COLS = _off


def _rms(x, g):
    return x * lax.rsqrt(jnp.mean(x * x, axis=-1, keepdims=True) + EPS) * g


def _dot(a, b):
    return jnp.dot(a, b, preferred_element_type=F32)


def _const_spec(shape):
    nd = len(shape)
    return pl.BlockSpec(shape, lambda *_: (0,) * nd, pipeline_mode=pl.Buffered(1))


def _swiglu_residual(x, pre_g, wg_ref, wu_ref, wd_ref, post_g, h_ref, fc):
    xn = _rms(x, pre_g).astype(BF16)
    for c in range(wg_ref.shape[1] // fc):
        sl = slice(c * fc, (c + 1) * fc)
        g = _dot(xn, wg_ref[:, sl])
        u = _dot(xn, wu_ref[:, sl])
        h_ref[:, sl] = (g * jax.nn.sigmoid(g) * u).astype(BF16)
    y = _dot(h_ref[...], wd_ref[...])
    return x + 0.5 * _rms(y, post_g)


def _ffn_kernel(x_ref, pre_g_ref, wg_ref, wu_ref, wd_ref, post_g_ref, o_ref, h_ref, *, fc):
    o_ref[...] = _swiglu_residual(x_ref[...], pre_g_ref[...], wg_ref, wu_ref, wd_ref,
                                  post_g_ref[...], h_ref, fc)


def _ffn_mix_kernel(x_ref, mla_ref, gdn_ref, mla_g_ref, wo_ref, mix_g_ref,
                    pre_g_ref, wg_ref, wu_ref, wd_ref, post_g_ref, o_ref, h_ref, *, fc):
    mla = _rms(mla_ref[...], mla_g_ref[...]).astype(BF16)
    mixed = _dot(mla, wo_ref[:MLA_W, :]) + _dot(gdn_ref[...].astype(BF16), wo_ref[MLA_W:, :])
    x2 = x_ref[...] + _rms(mixed, mix_g_ref[...])
    o_ref[...] = _swiglu_residual(x2, pre_g_ref[...], wg_ref, wu_ref, wd_ref,
                                  post_g_ref[...], h_ref, fc)


def _ffn_call(x, pre_g, wg, wu, wd, post_g, mix=None, *, tm=512, fc=256):
    n, d = x.shape
    f = wg.shape[1]
    row = lambda w: pl.BlockSpec((tm, w), lambda i: (i, 0))
    ffn_specs = [_const_spec((1, d)), _const_spec((d, f)), _const_spec((d, f)),
                 _const_spec((f, d)), _const_spec((1, d))]
    ffn_args = [pre_g, wg, wu, wd, post_g]
    if mix is None:
        body, in_specs, args = _ffn_kernel, [row(d)] + ffn_specs, [x] + ffn_args
    else:
        mla_o, gdn_o, mla_g, wo, mix_g = mix
        body = _ffn_mix_kernel
        in_specs = [row(d), row(MLA_W), row(GDN_W), _const_spec((1, MLA_W)),
                    _const_spec(wo.shape), _const_spec((1, d))] + ffn_specs
        args = [x, mla_o, gdn_o, mla_g, wo, mix_g] + ffn_args
    return pl.pallas_call(
        functools.partial(body, fc=fc),
        grid=(n // tm,),
        in_specs=in_specs,
        out_specs=row(d),
        out_shape=jax.ShapeDtypeStruct((n, d), F32),
        scratch_shapes=[pltpu.VMEM((tm, f), BF16)],
        compiler_params=pltpu.CompilerParams(dimension_semantics=("arbitrary",),
                                             vmem_limit_bytes=VMEM_LIMIT),
        name="ffn_mix" if mix is not None else "ffn",
    )(*args)


def _proj_kernel(x_ref, pos_ref, g_ref, w_ref, qg_ref, wq1_ref, wq2_ref, kvg_ref, wk_ref, wv_ref,
                 freq_ref, vones_ref, q_ref, k_ref, v_ref, qkv_ref, gate_ref, a_ref, b_ref):
    hn = _rms(x_ref[...], g_ref[...]).astype(BF16)

    def blk(name):
        lo, hi = _PROJ_OFF[name]
        return _dot(hn, w_ref[:, lo:hi])

    ang = pos_ref[...].astype(F32) * freq_ref[...]
    cos, sin = jnp.cos(ang), jnp.sin(ang)
    scale = (MLA_NOPE_DIM + MLA_ROPE_DIM) ** -0.5 * LOG2_E

    cq =_rms(blk("cq"), qg_ref[...]).astype(BF16)
    reps = (1, N_MLA_HEADS)
    q = _dot(cq, wq1_ref[...]) * jnp.tile(cos * scale, reps) + _dot(cq, wq2_ref[...]) * jnp.tile(sin * scale, reps)
    q_ref[...] = q.astype(BF16)

    ckv = _rms(blk("ckv"), kvg_ref[...]).astype(BF16)
    kpe = blk("kpe") * cos + blk("kper") * sin
    k_ref[...] = (_dot(ckv, wk_ref[...]) + jnp.tile(kpe, reps)).astype(BF16)
    v_ref[...] = (_dot(ckv, wv_ref[...]) + vones_ref[...]).astype(BF16)

    qkv_ref[...] = blk("qkv")
    gate_ref[...] = blk("gate")
    a_ref[...] = blk("a")
    b_ref[...] = blk("b")


def _proj_call(x1, pos, g, w_all, qg, wq1, wq2, kvg, wk, wv, freq, vones, *, tm=512):
    n, d = x1.shape
    row = lambda w: pl.BlockSpec((tm, w), lambda i: (i, 0))
    hw = N_MLA_HEADS * HEAD_PAD
    outs = [(hw, BF16), (hw, BF16), (hw, BF16), (3 * GDN_W, F32), (GDN_W, F32), (GDN_W, F32), (GDN_W, F32)]
    consts = [g, w_all, qg, wq1, wq2, kvg, wk, wv, freq, vones]
    return pl.pallas_call(
        _proj_kernel,
        grid=(n // tm,),
        in_specs=[row(d), row(1)] + [_const_spec(c.shape) for c in consts],
        out_specs=[row(w) for w, _ in outs],
        out_shape=[jax.ShapeDtypeStruct((n, w), dt) for w, dt in outs],
        compiler_params=pltpu.CompilerParams(dimension_semantics=("arbitrary",),
                                             vmem_limit_bytes=VMEM_LIMIT),
        name="proj",
    )(x1, pos, *consts)


def _attn_kernel(q_ref, k_ref, v_ref, o_ref, m_ref, acc_ref, *, seq, blk):
    lo = lax.broadcasted_iota(jnp.int32, (blk, LANES), 1) < MLA_V_DIM
    causal = (lax.broadcasted_iota(jnp.int32, (blk, blk), 1) <= lax.broadcasted_iota(jnp.int32, (blk, blk), 0))
    heads = [slice(hh * HEAD_PAD, (hh + 1) * HEAD_PAD) for hh in range(2)]

    def kv_step(qs, j, masked):
        rows = pl.ds(pl.multiple_of(j * blk, blk), blk)
        for hh, sl in enumerate(heads):
            s = lax.dot_general(qs[hh], k_ref[rows, sl], (((1,), (1,)), ((), ())), preferred_element_type=F32)
            if masked:
                s = jnp.where(causal, s, NEG_BIG)
            m_old = m_ref[hh]
            m_new = jnp.maximum(m_old, jnp.max(s, axis=-1, keepdims=True))
            p = jnp.exp2(s - jnp.tile(m_new, (1, blk // LANES)))
            acc_ref[hh] = jnp.exp2(m_old - m_new) * acc_ref[hh] + _dot(p.astype(BF16), v_ref[rows, sl])
            m_ref[hh] = m_new

    def q_block(qi, carry):
        rows = pl.ds(pl.multiple_of(qi * blk, blk), blk)
        qs = [q_ref[rows, sl] for sl in heads]
        m_ref[...] = jnp.full(m_ref.shape, NEG_BIG, F32)
        acc_ref[...] = jnp.zeros(acc_ref.shape, F32)

        def body(j, c):
            kv_step(qs, j, False)
            return c

        lax.fori_loop(0, qi, body, 0)
        kv_step(qs, qi, True)
        a0, a1 = acc_ref[0], acc_ref[1]
        den = pltpu.roll(jnp.where(lo, a1, a0), MLA_V_DIM, axis=1)
        o_ref[rows, :] = jnp.where(lo, a0, a1) / den
        return carry

    lax.fori_loop(0, seq // blk, q_block, 0)


def _attn_call(q, k, v, *, batch, seq, blk=512):
    pairs = N_MLA_HEADS // 2
    in_spec = pl.BlockSpec((seq, 2 * HEAD_PAD), lambda b, p: (b, p))
    return pl.pallas_call(
        functools.partial(_attn_kernel, seq=seq, blk=blk),
        grid=(batch, pairs),
        in_specs=[in_spec, in_spec, in_spec],
        out_specs=pl.BlockSpec((seq, LANES), lambda b, p: (b, p)),
        out_shape=jax.ShapeDtypeStruct((batch * seq, MLA_W), F32),
        scratch_shapes=[pltpu.VMEM((2, blk, LANES), F32), pltpu.VMEM((2, blk, LANES), F32)],
        compiler_params=pltpu.CompilerParams(dimension_semantics=("arbitrary", "arbitrary"),
                                             vmem_limit_bytes=VMEM_LIMIT),
        name="attn",
    )(q, k, v)


def _lo_mask(shape):
    return lax.broadcasted_iota(jnp.int32, shape, 1) % LANES < GDN_HEAD_DIM


def _pair_blockdiag(x_pair):
    lo = _lo_mask(x_pair.shape)
    zero = jnp.zeros_like(x_pair)
    return jnp.concatenate([jnp.where(lo, x_pair, zero), jnp.where(lo, zero, x_pair)], axis=0)


def _heads_matmul(a_all, b_all):
    outs = []
    for p in range(N_GDN_HEADS // 2):
        sl = slice(p * LANES, (p + 1) * LANES)
        outs.append(_dot(a_all[:, sl].astype(BF16), _pair_blockdiag(b_all[:, sl]).astype(BF16)))
    return jnp.concatenate(outs, axis=1)


def _heads_gram(a_all, b_all):
    outs = []
    for p in range(N_GDN_HEADS // 2):
        sl = slice(p * LANES, (p + 1) * LANES)
        outs.append(lax.dot_general(a_all[:, sl].astype(BF16), _pair_blockdiag(b_all[:, sl]).astype(BF16),
                                    (((1,), (1,)), ((), ())), preferred_element_type=F32))
    return jnp.concatenate(outs, axis=1)


def _heads_outer(a_all, b_all):
    outs = []
    for p in range(N_GDN_HEADS // 2):
        sl = slice(p * LANES, (p + 1) * LANES)
        full = lax.dot_general(a_all[:, sl].astype(BF16), b_all[:, sl].astype(BF16),
                               (((0,), (0,)), ((), ())), preferred_element_type=F32)
        outs.append(jnp.where(_lo_mask((GDN_HEAD_DIM, LANES)), full[:GDN_HEAD_DIM], full[GDN_HEAD_DIM:]))
    return jnp.concatenate(outs, axis=1)


def _unit_lower_inverse(l_all, row, col):
    eye = jnp.where(row == col, 1.0, 0.0).astype(F32)
    zero = jnp.zeros_like(l_all)
    same16 = (row // 16) == (col // 16)
    same32 = (row // 32) == (col // 32)
    mm = _heads_matmul
    m1 = jnp.where(same16, -l_all, zero)
    inv = eye + m1
    power = m1
    for _ in range(3):
        power = mm(power, power)
        inv = inv + mm(power, inv)
    for off in (jnp.where(same32 & ~same16, l_all, zero), jnp.where(same32, zero, l_all)):
        inv = inv - mm(inv, mm(off, inv))
    return inv


def _gdn_kernel(qkv_ref, gate_ref, a_ref, b_ref, convw_ref, alog_ref, dtb_ref, ng_ref, ones_ref,
                o_ref, xbuf_ref, s_ref, *, tm):
    c = GDN_CHUNK
    w = GDN_W
    halo = 8

    @pl.when(pl.program_id(1) == 0)
    def _():
        xbuf_ref[0:halo, :] = jnp.zeros((halo, 3 * w), F32)
        s_ref[...] = jnp.zeros(s_ref.shape, F32)

    xbuf_ref[halo:halo + tm, :] = qkv_ref[...]
    y = jnp.zeros((tm, 3 * w), F32)
    for kk in range(GDN_CONV):
        start = halo - (GDN_CONV - 1) + kk
        y = y + xbuf_ref[start:start + tm, :] * convw_ref[kk:kk + 1, :]
    xbuf_ref[0:halo, :] = xbuf_ref[tm:tm + halo, :]
    y = y * jax.nn.sigmoid(y)

    ones_bd = ones_ref[...]

    def head_sumsq(x):
        x2 = x * x
        hi = x2.astype(BF16)
        lo = (x2 - hi.astype(F32)).astype(BF16)
        return _dot(hi, ones_bd) + _dot(lo, ones_bd)

    q = y[:, 0:w]
    k = y[:, w:2 * w]
    v = y[:, 2 * w:3 * w]
    q = q * lax.rsqrt(head_sumsq(q) + EPS) * (GDN_HEAD_DIM ** -0.5)
    k = k * lax.rsqrt(head_sumsq(k) + EPS)

    sp_in = a_ref[...] + dtb_ref[...]
    softplus = jnp.maximum(sp_in, 0.0) + jnp.log1p(jnp.exp(-jnp.abs(sp_in)))
    g = -jnp.exp(alog_ref[...]) * softplus
    beta = jax.nn.sigmoid(b_ref[...])
    kb = k * beta
    vb = v * beta

    row = lax.broadcasted_iota(jnp.int32, (c, w), 0)
    col = lax.broadcasted_iota(jnp.int32, (c, w), 1) % GDN_HEAD_DIM
    tril = jnp.where(lax.broadcasted_iota(jnp.int32, (c, c), 1) <= lax.broadcasted_iota(jnp.int32, (c, c), 0),
                     1.0, 0.0).astype(BF16)
    zero = jnp.zeros((c, w), F32)

    def tril_dot(x):
        x1 = x.astype(BF16)
        r1 = x - x1.astype(F32)
        x2 = r1.astype(BF16)
        x3 = (r1 - x2.astype(F32)).astype(BF16)
        return _dot(tril, x1) + _dot(tril, x2) + _dot(tril, x3)

    outs = []
    for ci in range(tm // c):
        rs = slice(ci * c, (ci + 1) * c)
        sums = tril_dot(jnp.concatenate([g[rs], jnp.where(col < row, g[rs], zero)], axis=1))
        gc, span = sums[:, :w], sums[:, w:]
        decay = jnp.exp(jnp.where(col <= row, span, NEG_BIG))
        egc = jnp.exp(gc)
        gc_last = gc[c - 1:c, :]
        kc, qc = k[rs], q[rs]

        gram = _heads_gram(jnp.concatenate([kb[rs], qc], axis=0), kc)
        l_all = jnp.where(col < row, gram[:c] * decay, zero)
        attn = gram[c:] * decay
        t_all = _unit_lower_inverse(l_all, row, col)
        w_all = _heads_matmul(t_all, kb[rs] * egc)
        u_all = _heads_matmul(t_all, vb[rs])

        s_all = s_ref[...]
        ws_qs = _heads_matmul(jnp.concatenate([w_all, qc * egc], axis=0), s_all)
        v_new = u_all - ws_qs[:c]
        outs.append(ws_qs[c:] + _heads_matmul(attn, v_new))
        k_dec = kc * jnp.exp(gc_last - gc)
        s_ref[...] = s_all * jnp.exp(gc_last) + _heads_outer(k_dec, v_new)

    o = jnp.concatenate(outs, axis=0)
    var = head_sumsq(o) * (1.0 / GDN_HEAD_DIM)
    gate = gate_ref[...]
    o_ref[...] = o * lax.rsqrt(var + EPS) * ng_ref[...] * (gate * jax.nn.sigmoid(gate))


def _gdn_call(qkv, gate, a_e, b_e, convw, alog_e, dtb_e, ng_e, ones_bd, *, batch, seq, tm=256):
    nt = seq // tm
    row = lambda wd: pl.BlockSpec((tm, wd), lambda b, j: (b * nt + j, 0))
    consts = [convw, alog_e, dtb_e, ng_e, ones_bd]
    return pl.pallas_call(
        functools.partial(_gdn_kernel, tm=tm),
        grid=(batch, nt),
        in_specs=[row(3 * GDN_W), row(GDN_W), row(GDN_W), row(GDN_W)] + [_const_spec(x.shape) for x in consts],
        out_specs=row(GDN_W),
        out_shape=jax.ShapeDtypeStruct((batch * seq, GDN_W), F32),
        scratch_shapes=[pltpu.VMEM((tm + 8, 3 * GDN_W), F32), pltpu.VMEM((GDN_HEAD_DIM, GDN_W), F32)],
        compiler_params=pltpu.CompilerParams(dimension_semantics=("arbitrary", "arbitrary"),
                                             vmem_limit_bytes=VMEM_LIMIT),
        name="gdn",
    )(qkv, gate, a_e, b_e, *consts)


def _rotate_half_cols(w):
    half = MLA_ROPE_DIM // 2
    return jnp.concatenate([-w[:, half:], w[:, :half]], axis=1)


def _prep_proj_weights(w_in, w_uq, w_ukv):
    d = w_in.shape[0]
    sizes = (MLA_Q_RANK, MLA_KV_RANK, MLA_ROPE_DIM, 3 * GDN_W, N_GDN_HEADS, N_GDN_HEADS, GDN_W)
    cuts = np.cumsum(sizes)[:-1]
    w_cq, w_ckv, w_kpe, w_qkv, w_a, w_b, w_gate = jnp.split(w_in, cuts, axis=1)

    def rope_block(wpe):
        z_lo = jnp.zeros((d, MLA_NOPE_DIM), F32)
        z_hi = jnp.zeros((d, HEAD_PAD - MLA_NOPE_DIM - MLA_ROPE_DIM), F32)
        return jnp.concatenate([z_lo, wpe, z_hi], axis=1)

    w_all = jnp.concatenate([
        w_cq, w_ckv, rope_block(w_kpe), rope_block(_rotate_half_cols(w_kpe)), w_qkv, w_gate,
        jnp.repeat(w_a, GDN_HEAD_DIM, axis=1), jnp.repeat(w_b, GDN_HEAD_DIM, axis=1)], axis=1).astype(BF16)

    qh = w_uq.reshape(MLA_Q_RANK, N_MLA_HEADS, MLA_NOPE_DIM + MLA_ROPE_DIM)
    q_nope, q_pe = qh[..., :MLA_NOPE_DIM], qh[..., MLA_NOPE_DIM:]
    q_pe_rot = jnp.concatenate([-q_pe[..., MLA_ROPE_DIM // 2:], q_pe[..., :MLA_ROPE_DIM // 2]], axis=-1)
    pad = jnp.zeros((MLA_Q_RANK, N_MLA_HEADS, HEAD_PAD - MLA_NOPE_DIM - MLA_ROPE_DIM), F32)
    wq1 = jnp.concatenate([q_nope, q_pe, pad], axis=-1).reshape(MLA_Q_RANK, -1).astype(BF16)
    wq2 = jnp.concatenate([jnp.zeros_like(q_nope), q_pe_rot, pad], axis=-1).reshape(MLA_Q_RANK, -1).astype(BF16)

    kvh = w_ukv.reshape(MLA_KV_RANK, N_MLA_HEADS, MLA_NOPE_DIM + MLA_V_DIM)
    k_nope, v = kvh[..., :MLA_NOPE_DIM], kvh[..., MLA_NOPE_DIM:]
    kpad = jnp.zeros((MLA_KV_RANK, N_MLA_HEADS, HEAD_PAD - MLA_NOPE_DIM), F32)
    wk = jnp.concatenate([k_nope, kpad], axis=-1).reshape(MLA_KV_RANK, -1).astype(BF16)
    vp = v.reshape(MLA_KV_RANK, N_MLA_HEADS // 2, 2, MLA_V_DIM)
    vz = jnp.zeros_like(vp[:, :, 0])
    wv = jnp.concatenate([vp[:, :, 0], vz, vz, vp[:, :, 1]], axis=-1).reshape(MLA_KV_RANK, -1).astype(BF16)
    return w_all, wq1, wq2, wk, wv


def _value_ones_lanes():
    pair = jnp.concatenate([jnp.zeros((MLA_V_DIM,), F32), jnp.ones((2 * MLA_V_DIM,), F32),
                            jnp.zeros((MLA_V_DIM,), F32)])
    return jnp.tile(pair, N_MLA_HEADS // 2)[None, :]


def _rope_lane_freqs():
    half = MLA_ROPE_DIM // 2
    freqs = ROPE_THETA ** (-jnp.arange(half, dtype=F32) / half)
    return jnp.concatenate([jnp.zeros((MLA_NOPE_DIM,), F32), freqs, freqs,
                            jnp.zeros((HEAD_PAD - MLA_NOPE_DIM - MLA_ROPE_DIM,), F32)])[None, :]


def kernel(x, positions, ffn1_pre_g, ffn1_w_gate, ffn1_w_up, ffn1_w_down, ffn1_post_g, mix_pre_g, w_in, mla_q_norm_g, mla_w_uq, mla_kv_norm_g, mla_w_ukv, mla_out_g, gdn_conv_w, gdn_a_log, gdn_dt_bias, gdn_norm_g, w_out, mix_post_g, ffn2_pre_g, ffn2_w_gate, ffn2_w_up, ffn2_w_down, ffn2_post_g):
    batch, seq, d = x.shape
    n = batch * seq
    xt = x.reshape(n, d)
    pos = positions.reshape(n, 1)
    freq = _rope_lane_freqs()
    head_ids = jnp.arange(GDN_W) // GDN_HEAD_DIM
    ones_bd = (head_ids[:, None] == head_ids[None, :]).astype(BF16)
    bf = lambda w: w.astype(BF16)
    r = lambda g: g[None, :]

    for l in range(ffn1_pre_g.shape[0]):
        xt = _ffn_call(xt, r(ffn1_pre_g[l]), bf(ffn1_w_gate[l]), bf(ffn1_w_up[l]), bf(ffn1_w_down[l]),
                       r(ffn1_post_g[l]))
        w_all, wq1, wq2, wk, wv = _prep_proj_weights(w_in[l], mla_w_uq[l], mla_w_ukv[l])
        q, k, v, qkv, gate, a_e, b_e = _proj_call(xt, pos, r(mix_pre_g[l]), w_all, r(mla_q_norm_g[l]), wq1, wq2,
                                                  r(mla_kv_norm_g[l]), wk, wv, freq, _value_ones_lanes())
        mla_o = _attn_call(q, k, v, batch=batch, seq=seq)
        gdn_o = _gdn_call(qkv, gate, a_e, b_e, gdn_conv_w[l],
                          r(jnp.repeat(gdn_a_log[l].astype(F32), GDN_HEAD_DIM)),
                          r(jnp.repeat(gdn_dt_bias[l].astype(F32), GDN_HEAD_DIM)),
                          r(jnp.tile(gdn_norm_g[l], N_GDN_HEADS)), ones_bd, batch=batch, seq=seq)
        xt = _ffn_call(xt, r(ffn2_pre_g[l]), bf(ffn2_w_gate[l]), bf(ffn2_w_up[l]), bf(ffn2_w_down[l]),
                       r(ffn2_post_g[l]),
                       mix=(mla_o, gdn_o, r(mla_out_g[l]), bf(w_out[l]), r(mix_post_g[l])))
    return xt.reshape(batch, seq, d)
```

```python
import functools

import jax
import jax.numpy as jnp
import numpy as np
from jax import lax
from jax.experimental import pallas as pl
from jax.experimental.pallas import tpu as pltpu

F32 = jnp.float32
BF16 = jnp.bfloat16

D_MODEL = 1024
N_MLA_HEADS = 8
MLA_Q_RANK = 256
MLA_KV_RANK = 128
MLA_NOPE_DIM = 64
MLA_ROPE_DIM = 32
MLA_V_DIM = 64
ROPE_THETA = 10000.0
N_GDN_HEADS = 8
GDN_HEAD_DIM = 64
GDN_CONV = 4
GDN_CHUNK = 64
MLA_W = N_MLA_HEADS * MLA_V_DIM
GDN_W = N_GDN_HEADS * GDN_HEAD_DIM
D_FF = 2816
EPS = 1e-6

LANES = 128
POS_PACK = LANES // (MLA_ROPE_DIM // 2)
HEAD_PAD = 128
NEG_BIG = -1e30
LOG2_E = 1.4426950408889634
CONV_HALO = 8
VMEM_LIMIT = 56 * 1024 * 1024

_PROJ_BLOCKS = (("cq", MLA_Q_RANK), ("ckv", MLA_KV_RANK), ("kpe", LANES), ("kper", LANES),
                ("qkv", 3 * GDN_W), ("gate", GDN_W), ("a", GDN_W), ("b", GDN_W))
_PROJ_OFF = {}
_off = 0
for _name, _width in _PROJ_BLOCKS:
    _PROJ_OFF[_name] = (_off, _off + _width)
    _off += _width
PROJ_COLS = _off


def _rms(x, g):
    return x * lax.rsqrt(jnp.mean(x * x, axis=-1, keepdims=True) + EPS) * g


def _dot(a, b):
    return jnp.dot(a, b, preferred_element_type=F32)


def _const_spec(shape):
    nd = len(shape)
    return pl.BlockSpec(shape, lambda *_: (0,) * nd, pipeline_mode=pl.Buffered(1))


def _swiglu_residual(x, pre_g, wg_ref, wu_ref, wd_ref, post_g, h_ref, fc):
    xn = _rms(x, pre_g).astype(BF16)
    for c in range(wg_ref.shape[1] // fc):
        sl = slice(c * fc, (c + 1) * fc)
        g = _dot(xn, wg_ref[:, sl])
        u = _dot(xn, wu_ref[:, sl])
        h_ref[:, sl] = (g * jax.nn.sigmoid(g) * u).astype(BF16)
    y = _dot(h_ref[...], wd_ref[...])
    return x + 0.5 * _rms(y, post_g)


def _ffn_kernel(x_ref, pre_g_ref, wg_ref, wu_ref, wd_ref, post_g_ref, o_ref, h_ref, *, fc):
    o_ref[...] = _swiglu_residual(x_ref[...], pre_g_ref[...], wg_ref, wu_ref, wd_ref,
                                  post_g_ref[...], h_ref, fc)


def _ffn_mix_kernel(x_ref, mla_ref, gdn_ref, mla_g_ref, wo_ref, mix_g_ref,
                    pre_g_ref, wg_ref, wu_ref, wd_ref, post_g_ref, o_ref, h_ref, *, fc):
    mla = _rms(mla_ref[...], mla_g_ref[...]).astype(BF16)
    mixed = _dot(mla, wo_ref[:MLA_W, :]) + _dot(gdn_ref[...], wo_ref[MLA_W:, :])
    x2 = x_ref[...] + _rms(mixed, mix_g_ref[...])
    o_ref[...] = _swiglu_residual(x2, pre_g_ref[...], wg_ref, wu_ref, wd_ref,
                                  post_g_ref[...], h_ref, fc)


def _ffn_call(x, pre_g, wg, wu, wd, post_g, mix=None, *, tm=512, fc=256):
    n, d = x.shape
    f = wg.shape[1]
    row = lambda w: pl.BlockSpec((tm, w), lambda i: (i, 0))
    ffn_specs = [_const_spec((1, d)), _const_spec((d, f)), _const_spec((d, f)),
                 _const_spec((f, d)), _const_spec((1, d))]
    ffn_args = [pre_g, wg, wu, wd, post_g]
    if mix is None:
        body, in_specs, args = _ffn_kernel, [row(d)] + ffn_specs, [x] + ffn_args
    else:
        mla_o, gdn_o, mla_g, wo, mix_g = mix
        body = _ffn_mix_kernel
        in_specs = [row(d), row(MLA_W), row(GDN_W), _const_spec((1, MLA_W)),
                    _const_spec(wo.shape), _const_spec((1, d))] + ffn_specs
        args = [x, mla_o, gdn_o, mla_g, wo, mix_g] + ffn_args
    return pl.pallas_call(
        functools.partial(body, fc=fc),
        grid=(n // tm,),
        in_specs=in_specs,
        out_specs=row(d),
        out_shape=jax.ShapeDtypeStruct((n, d), F32),
        scratch_shapes=[pltpu.VMEM((tm, f), BF16)],
        compiler_params=pltpu.CompilerParams(dimension_semantics=("arbitrary",),
                                             vmem_limit_bytes=VMEM_LIMIT),
        name="ffn_mix" if mix is not None else "ffn",
    )(*args)


def _proj_kernel(x_ref, pos_ref, pre_g_ref, w_ref, qg_ref, wq1_ref, wq2_ref, kvg_ref, wk_ref, wv_ref,
                 freq_ref, spread_ref, nonrope_ref, vones_ref, convw_ref, alog_ref, dtb_ref, ones_ref,
                 q_ref, k_ref, v_ref, gq_ref, gk_ref, gkb_ref, gvb_ref, g_ref, gate_ref,
                 xbuf_ref, *, tiles_per_seq):
    hn = _rms(x_ref[...], pre_g_ref[...]).astype(BF16)

    def blk(name):
        lo, hi = _PROJ_OFF[name]
        return _dot(hn, w_ref[:, lo:hi])

    tm = x_ref.shape[0]
    ang = pos_ref[...].astype(F32) * freq_ref[...]
    tok = lax.broadcasted_iota(jnp.int32, (tm, LANES), 0) % POS_PACK
    own = lax.broadcasted_iota(jnp.int32, (tm, LANES), 1) // (MLA_ROPE_DIM // 2) == tok

    def to_rope_lanes(packed):
        per_tok = jnp.broadcast_to(packed[:, None, :], (tm // POS_PACK, POS_PACK, LANES)).reshape(tm, LANES)
        per_tok = jnp.where(own, per_tok, 0.0)
        hi = per_tok.astype(BF16)
        lo = (per_tok - hi.astype(F32)).astype(BF16)
        return _dot(hi, spread_ref[...]) + _dot(lo, spread_ref[...])

    sin = to_rope_lanes(jnp.sin(ang))
    cos = to_rope_lanes(jnp.cos(ang)) + nonrope_ref[...]
    scale = (MLA_NOPE_DIM + MLA_ROPE_DIM) ** -0.5 * LOG2_E

    cq =_rms(blk("cq"), qg_ref[...]).astype(BF16)
    reps = (1, N_MLA_HEADS)
    q = _dot(cq, wq1_ref[...]) * jnp.tile(cos * scale, reps) + _dot(cq, wq2_ref[...]) * jnp.tile(sin * scale, reps)
    q_ref[...] = q.astype(BF16)

    ckv = _rms(blk("ckv"), kvg_ref[...]).astype(BF16)
    kpe = blk("kpe") * cos + blk("kper") * sin
    k_ref[...] = (_dot(ckv, wk_ref[...]) + jnp.tile(kpe, reps)).astype(BF16)
    v_ref[...] = (_dot(ckv, wv_ref[...]) + vones_ref[...]).astype(BF16)

    w = GDN_W

    @pl.when(pl.program_id(0) % tiles_per_seq == 0)
    def _():
        xbuf_ref[0:CONV_HALO, :] = jnp.zeros((CONV_HALO, 3 * w), F32)

    xbuf_ref[CONV_HALO:CONV_HALO + tm, :] = blk("qkv")
    xe = xbuf_ref[...]
    y = xe[CONV_HALO:, :] * convw_ref[GDN_CONV - 1:GDN_CONV, :]
    for back in range(1, GDN_CONV):
        shifted = pltpu.roll(xe, back, axis=0)[CONV_HALO:, :]
        y = y + shifted * convw_ref[GDN_CONV - 1 - back:GDN_CONV - back, :]
    xbuf_ref[0:CONV_HALO, :] = xe[tm:, :]
    y = y * jax.nn.sigmoid(y)

    def head_sumsq(x):
        return _dot((x * x).astype(BF16), ones_ref[...])

    gq, gk, gv = y[:, 0:w], y[:, w:2 * w], y[:, 2 * w:3 * w]
    gq_ref[...] = (gq * lax.rsqrt(head_sumsq(gq) + EPS) * (GDN_HEAD_DIM ** -0.5)).astype(BF16)
    gk = gk * lax.rsqrt(head_sumsq(gk) + EPS)
    gk_ref[...] = gk.astype(BF16)
    beta = jax.nn.sigmoid(blk("b"))
    gkb_ref[...] = (gk * beta).astype(BF16)
    gvb_ref[...] = (gv * beta).astype(BF16)
    sp_in = blk("a") + dtb_ref[...]
    softplus = jnp.maximum(sp_in, 0.0) + jnp.log1p(jnp.exp(-jnp.abs(sp_in)))
    g_ref[...] = -jnp.exp(alog_ref[...]) * softplus
    gate = blk("gate")
    gate_ref[...] = gate * jax.nn.sigmoid(gate)


def _proj_call(x1, pos, consts, *, seq, tm=512):
    n, d = x1.shape
    row = lambda w: pl.BlockSpec((tm, w), lambda i: (i, 0))
    hw = N_MLA_HEADS * HEAD_PAD
    outs = [(hw, BF16), (hw, BF16), (hw, BF16),
            (GDN_W, BF16), (GDN_W, BF16), (GDN_W, BF16), (GDN_W, BF16), (GDN_W, F32), (GDN_W, F32)]
    return pl.pallas_call(
        functools.partial(_proj_kernel, tiles_per_seq=seq // tm),
        grid=(n // tm,),
        in_specs=([row(d), pl.BlockSpec((tm // POS_PACK, LANES), lambda i: (i, 0))]
                  + [_const_spec(c.shape) for c in consts]),
        out_specs=[row(w) for w, _ in outs],
        out_shape=[jax.ShapeDtypeStruct((n, w), dt) for w, dt in outs],
        scratch_shapes=[pltpu.VMEM((tm + CONV_HALO, 3 * GDN_W), F32)],
        compiler_params=pltpu.CompilerParams(dimension_semantics=("arbitrary",),
                                             vmem_limit_bytes=VMEM_LIMIT),
        name="proj",
    )(x1, pos, *consts)


def _attn_kernel(q_ref, k_ref, v_ref, o_ref, m_ref, acc_ref, s_ref, *, seq, bq):
    bk = bq // 2
    lo = lax.broadcasted_iota(jnp.int32, (bq, LANES), 1) < MLA_V_DIM
    heads = [slice(hh * HEAD_PAD, (hh + 1) * HEAD_PAD) for hh in range(2)]

    def kv_rows(j):
        return pl.ds(pl.multiple_of(j * bk, bk), bk)

    def scores(qs, j, buf, q_rows=slice(None)):
        for hh, sl in enumerate(heads):
            s_ref[buf, hh, q_rows] = lax.dot_general(qs[hh][q_rows], k_ref[kv_rows(j), sl],
                                                     (((1,), (1,)), ((), ())), preferred_element_type=F32)

    def accumulate(j, buf, q_rows=slice(None), mask=None):
        for hh, sl in enumerate(heads):
            s = s_ref[buf, hh, q_rows]
            if mask is not None:
                s = jnp.where(mask, s, NEG_BIG)
            m_old = m_ref[hh, q_rows]
            m_new = jnp.maximum(m_old, jnp.max(s, axis=-1, keepdims=True))
            p = jnp.exp2(s - jnp.tile(m_new, (1, bk // LANES)))
            acc_ref[hh, q_rows] = (jnp.exp2(m_old - m_new) * acc_ref[hh, q_rows]
                                   + _dot(p.astype(BF16), v_ref[kv_rows(j), sl]))
            m_ref[hh, q_rows] = m_new

    def q_block(qi, carry):
        rows = pl.ds(pl.multiple_of(qi * bq, bq), bq)
        qs = [q_ref[rows, sl] for sl in heads]
        m_ref[...] = jnp.full(m_ref.shape, NEG_BIG, F32)
        acc_ref[...] = jnp.zeros(acc_ref.shape, F32)
        scores(qs, 0, 0)

        def body(t, c):
            scores(qs, 2 * t + 1, 1)
            accumulate(2 * t, 0)
            scores(qs, 2 * t + 2, 0)
            accumulate(2 * t + 1, 1)
            return c

        lax.fori_loop(0, qi, body, 0)
        scores(qs, 2 * qi + 1, 1, q_rows=slice(bk, bq))
        accumulate(2 * qi, 0, mask=(lax.broadcasted_iota(jnp.int32, (bq, bk), 1)
                                    <= lax.broadcasted_iota(jnp.int32, (bq, bk), 0)))
        accumulate(2 * qi + 1, 1, q_rows=slice(bk, bq),
                   mask=(lax.broadcasted_iota(jnp.int32, (bk, bk), 1)
                         <= lax.broadcasted_iota(jnp.int32, (bk, bk), 0)))
        a0, a1 = acc_ref[0], acc_ref[1]
        den = pltpu.roll(jnp.where(lo, a1, a0), MLA_V_DIM, axis=1)
        o_ref[rows, :] = jnp.where(lo, a0, a1) / den
        return carry

    lax.fori_loop(0, seq // bq, q_block, 0)


def _attn_call(q, k, v, *, batch, seq, bq=512):
    pairs = N_MLA_HEADS // 2
    in_spec = pl.BlockSpec((seq, 2 * HEAD_PAD), lambda b, p: (b, p))
    return pl.pallas_call(
        functools.partial(_attn_kernel, seq=seq, bq=bq),
        grid=(batch, pairs),
        in_specs=[in_spec, in_spec, in_spec],
        out_specs=pl.BlockSpec((seq, LANES), lambda b, p: (b, p)),
        out_shape=jax.ShapeDtypeStruct((batch * seq, MLA_W), F32),
        scratch_shapes=[pltpu.VMEM((2, bq, LANES), F32), pltpu.VMEM((2, bq, LANES), F32),
                        pltpu.VMEM((2, 2, bq, bq // 2), F32)],
        compiler_params=pltpu.CompilerParams(dimension_semantics=("arbitrary", "arbitrary"),
                                             vmem_limit_bytes=VMEM_LIMIT),
        name="attn",
    )(q, k, v)


GROUP_HEADS = LANES // GDN_HEAD_DIM
GROUP_LANES = GROUP_HEADS * GDN_HEAD_DIM
GROUP_SLICES = tuple(slice(i * GROUP_LANES, (i + 1) * GROUP_LANES) for i in range(GDN_W // GROUP_LANES))


def _head_in_group(shape, dtype=jnp.int32):
    lane = lax.broadcasted_iota(jnp.int32, shape, 1) % GROUP_LANES // GDN_HEAD_DIM
    return lane.astype(F32).astype(dtype)


def _group_blockdiag(x_grp):
    x_grp = x_grp.astype(BF16)
    head = _head_in_group(x_grp.shape, BF16)
    zero = jnp.zeros_like(x_grp)
    return jnp.concatenate([jnp.where(head == h, x_grp, zero) for h in range(GROUP_HEADS)], axis=0)


def _heads_matmul(a_all, b_all):
    return jnp.concatenate([_dot(a_all[:, sl].astype(BF16), _group_blockdiag(b_all[:, sl]))
                            for sl in GROUP_SLICES], axis=1)


def _heads_gram(a_all, b_all):
    return jnp.concatenate([lax.dot_general(a_all[:, sl].astype(BF16), _group_blockdiag(b_all[:, sl]),
                                            (((1,), (1,)), ((), ())), preferred_element_type=F32)
                            for sl in GROUP_SLICES], axis=1)


def _heads_outer(a_all, b_all):
    outs = []
    head = _head_in_group((GDN_HEAD_DIM, GROUP_LANES))
    for sl in GROUP_SLICES:
        full = lax.dot_general(a_all[:, sl].astype(BF16), b_all[:, sl].astype(BF16),
                               (((0,), (0,)), ((), ())), preferred_element_type=F32)
        diag = full[:GDN_HEAD_DIM]
        for h in range(1, GROUP_HEADS):
            diag = jnp.where(head == h, full[h * GDN_HEAD_DIM:(h + 1) * GDN_HEAD_DIM], diag)
        outs.append(diag)
    return jnp.concatenate(outs, axis=1)


def _unit_lower_inverse(l_all, row, col):
    eye = jnp.where(row == col, 1.0, 0.0).astype(F32)
    zero = jnp.zeros_like(row, dtype=F32)
    same16 = (row // 16) == (col // 16)
    same32 = (row // 32) == (col // 32)
    mm = lambda xs, ys: [_heads_matmul(x, y) for x, y in zip(xs, ys)]
    power = [jnp.where(same16, -l, zero) for l in l_all]
    inv = [eye + m for m in power]
    for _ in range(3):
        power = mm(power, power)
        inv = [i + pi for i, pi in zip(inv, mm(power, inv))]
    for pick in (same32 & ~same16, ~same32):
        off = [jnp.where(pick, l, zero) for l in l_all]
        inv = [i - x for i, x in zip(inv, mm(inv, mm(off, inv)))]
    return inv


def _gdn_kernel(q_ref, k_ref, kb_ref, vb_ref, g_ref, gate_ref, ng_ref, ones_ref, o_ref, s_ref, *, tm):
    c = GDN_CHUNK
    w = GDN_W

    @pl.when(pl.program_id(1) == 0)
    def _():
        s_ref[...] = jnp.zeros(s_ref.shape, F32)

    row = lax.broadcasted_iota(jnp.int32, (c, w), 0)
    col = lax.broadcasted_iota(jnp.int32, (c, w), 1) % GDN_HEAD_DIM
    tril = jnp.where(lax.broadcasted_iota(jnp.int32, (c, c), 1) <= lax.broadcasted_iota(jnp.int32, (c, c), 0),
                     1.0, 0.0).astype(BF16)
    zero = jnp.zeros((c, w), F32)

    def tril_dot(x):
        hi = x.astype(BF16)
        lo = (x - hi.astype(F32)).astype(BF16)
        return _dot(tril, hi) + _dot(tril, lo)

    chunks = [slice(ci * c, (ci + 1) * c) for ci in range(tm // c)]
    qs, ks, kbs = ([ref[rs, :] for rs in chunks] for ref in (q_ref, k_ref, kb_ref))
    gs = [g_ref[rs, :] for rs in chunks]
    sums = [tril_dot(jnp.concatenate([g, jnp.where(col < row, g, zero)], axis=1)) for g in gs]
    gcs = [sm[:, :w] for sm in sums]
    decays = [jnp.exp(jnp.where(col <= row, sm[:, w:], NEG_BIG)) for sm in sums]
    egcs = [jnp.exp(gc) for gc in gcs]
    grams = [_heads_gram(jnp.concatenate([kb, qc], axis=0), kc) for kb, qc, kc in zip(kbs, qs, ks)]
    t_alls = _unit_lower_inverse([jnp.where(col < row, gr[:c] * dc, zero) for gr, dc in zip(grams, decays)],
                                 row, col)
    attns = [gr[c:] * dc for gr, dc in zip(grams, decays)]
    w_alls = [_heads_matmul(t, kb.astype(F32) * e) for t, kb, e in zip(t_alls, kbs, egcs)]
    u_alls = [_heads_matmul(t, vb_ref[rs, :]) for t, rs in zip(t_alls, chunks)]
    q_decs = [qc.astype(F32) * e for qc, e in zip(qs, egcs)]
    k_decs = [kc.astype(F32) * jnp.exp(gc[c - 1:c, :] - gc) for kc, gc in zip(ks, gcs)]
    s_decays = [jnp.exp(gc[c - 1:c, :]) for gc in gcs]

    s_all = s_ref[...]
    outs = []
    for ci in range(tm // c):
        ws_qs = _heads_matmul(jnp.concatenate([w_alls[ci], q_decs[ci]], axis=0), s_all)
        v_new = u_alls[ci] - ws_qs[:c]
        outs.append(ws_qs[c:] + _heads_matmul(attns[ci], v_new))
        s_all = s_all * s_decays[ci] + _heads_outer(k_decs[ci], v_new)
    s_ref[...] = s_all

    o = jnp.concatenate(outs, axis=0)
    var = _dot((o * o).astype(BF16), ones_ref[...]) * (1.0 / GDN_HEAD_DIM)
    o_ref[...] = (o * lax.rsqrt(var + EPS) * ng_ref[...] * gate_ref[...]).astype(BF16)


def _gdn_call(gq, gk, gkb, gvb, g_e, gate, ng_e, ones_bd, *, batch, seq, tm=512):
    nt = seq // tm
    row = pl.BlockSpec((tm, GDN_W), lambda b, j: (b * nt + j, 0))
    consts = [ng_e, ones_bd]
    return pl.pallas_call(
        functools.partial(_gdn_kernel, tm=tm),
        grid=(batch, nt),
        in_specs=[row] * 6 + [_const_spec(x.shape) for x in consts],
        out_specs=row,
        out_shape=jax.ShapeDtypeStruct((batch * seq, GDN_W), BF16),
        scratch_shapes=[pltpu.VMEM((GDN_HEAD_DIM, GDN_W), F32)],
        compiler_params=pltpu.CompilerParams(dimension_semantics=("arbitrary", "arbitrary"),
                                             vmem_limit_bytes=VMEM_LIMIT),
        name="gdn",
    )(gq, gk, gkb, gvb, g_e, gate, *consts)


def _rotate_half_cols(w):
    half = MLA_ROPE_DIM // 2
    return jnp.concatenate([-w[:, half:], w[:, :half]], axis=1)


def _prep_proj_weights(w_in, w_uq, w_ukv):
    d = w_in.shape[0]
    sizes = (MLA_Q_RANK, MLA_KV_RANK, MLA_ROPE_DIM, 3 * GDN_W, N_GDN_HEADS, N_GDN_HEADS, GDN_W)
    cuts = np.cumsum(sizes)[:-1]
    w_cq, w_ckv, w_kpe, w_qkv, w_a, w_b, w_gate = jnp.split(w_in, cuts, axis=1)

    def rope_block(wpe):
        z_lo = jnp.zeros((d, MLA_NOPE_DIM), F32)
        z_hi = jnp.zeros((d, HEAD_PAD - MLA_NOPE_DIM - MLA_ROPE_DIM), F32)
        return jnp.concatenate([z_lo, wpe, z_hi], axis=1)

    w_all = jnp.concatenate([
        w_cq, w_ckv, rope_block(w_kpe), rope_block(_rotate_half_cols(w_kpe)), w_qkv, w_gate,
        jnp.repeat(w_a, GDN_HEAD_DIM, axis=1), jnp.repeat(w_b, GDN_HEAD_DIM, axis=1)], axis=1).astype(BF16)

    qh = w_uq.reshape(MLA_Q_RANK, N_MLA_HEADS, MLA_NOPE_DIM + MLA_ROPE_DIM)
    q_nope, q_pe = qh[..., :MLA_NOPE_DIM], qh[..., MLA_NOPE_DIM:]
    q_pe_rot = jnp.concatenate([-q_pe[..., MLA_ROPE_DIM // 2:], q_pe[..., :MLA_ROPE_DIM // 2]], axis=-1)
    pad = jnp.zeros((MLA_Q_RANK, N_MLA_HEADS, HEAD_PAD - MLA_NOPE_DIM - MLA_ROPE_DIM), F32)
    wq1 = jnp.concatenate([q_nope, q_pe, pad], axis=-1).reshape(MLA_Q_RANK, -1).astype(BF16)
    wq2 = jnp.concatenate([jnp.zeros_like(q_nope), q_pe_rot, pad], axis=-1).reshape(MLA_Q_RANK, -1).astype(BF16)

    kvh = w_ukv.reshape(MLA_KV_RANK, N_MLA_HEADS, MLA_NOPE_DIM + MLA_V_DIM)
    k_nope, v = kvh[..., :MLA_NOPE_DIM], kvh[..., MLA_NOPE_DIM:]
    kpad = jnp.zeros((MLA_KV_RANK, N_MLA_HEADS, HEAD_PAD - MLA_NOPE_DIM), F32)
    wk = jnp.concatenate([k_nope, kpad], axis=-1).reshape(MLA_KV_RANK, -1).astype(BF16)
    vp = v.reshape(MLA_KV_RANK, N_MLA_HEADS // 2, 2, MLA_V_DIM)
    vz = jnp.zeros_like(vp[:, :, 0])
    wv = jnp.concatenate([vp[:, :, 0], vz, vz, vp[:, :, 1]], axis=-1).reshape(MLA_KV_RANK, -1).astype(BF16)
    return w_all, wq1, wq2, wk, wv


def _value_ones_lanes():
    pair = jnp.concatenate([jnp.zeros((MLA_V_DIM,), F32), jnp.ones((2 * MLA_V_DIM,), F32),
                            jnp.zeros((MLA_V_DIM,), F32)])
    return jnp.tile(pair, N_MLA_HEADS // 2)[None, :]


def _rope_constants():
    half = MLA_ROPE_DIM // 2
    freqs = ROPE_THETA ** (-jnp.arange(half, dtype=F32) / half)
    freq_packed = jnp.tile(freqs, LANES // half)[None, :]
    j = jnp.arange(LANES) % half
    lane = jnp.arange(LANES)
    spread = ((lane[None, :] == MLA_NOPE_DIM + j[:, None])
              | (lane[None, :] == MLA_NOPE_DIM + half + j[:, None])).astype(BF16)
    nonrope = ((lane < MLA_NOPE_DIM) | (lane >= MLA_NOPE_DIM + MLA_ROPE_DIM)).astype(F32)[None, :]
    return freq_packed, spread, nonrope


def kernel(x, positions, ffn1_pre_g, ffn1_w_gate, ffn1_w_up, ffn1_w_down, ffn1_post_g, mix_pre_g, w_in, mla_q_norm_g, mla_w_uq, mla_kv_norm_g, mla_w_ukv, mla_out_g, gdn_conv_w, gdn_a_log, gdn_dt_bias, gdn_norm_g, w_out, mix_post_g, ffn2_pre_g, ffn2_w_gate, ffn2_w_up, ffn2_w_down, ffn2_post_g):
    batch, seq, d = x.shape
    n = batch * seq
    xt = x.reshape(n, d)
    pos = jnp.repeat(positions.reshape(n), MLA_ROPE_DIM // 2).reshape(n // POS_PACK, LANES)
    freq, spread, nonrope = _rope_constants()
    head_ids = jnp.arange(GDN_W) // GDN_HEAD_DIM
    ones_bd = (head_ids[:, None] == head_ids[None, :]).astype(BF16)
    bf = lambda w: w.astype(BF16)
    r = lambda g: g[None, :]

    for l in range(ffn1_pre_g.shape[0]):
        xt = _ffn_call(xt, r(ffn1_pre_g[l]), bf(ffn1_w_gate[l]), bf(ffn1_w_up[l]), bf(ffn1_w_down[l]),
                       r(ffn1_post_g[l]))
        w_all, wq1, wq2, wk, wv = _prep_proj_weights(w_in[l], mla_w_uq[l], mla_w_ukv[l])
        proj_consts = [r(mix_pre_g[l]), w_all, r(mla_q_norm_g[l]), wq1, wq2, r(mla_kv_norm_g[l]), wk, wv,
                       freq, spread, nonrope, _value_ones_lanes(), gdn_conv_w[l],
                       r(jnp.repeat(gdn_a_log[l].astype(F32), GDN_HEAD_DIM)),
                       r(jnp.repeat(gdn_dt_bias[l].astype(F32), GDN_HEAD_DIM)), ones_bd]
        q, k, v, gq, gk, gkb, gvb, g_e, gate = _proj_call(xt, pos, proj_consts, seq=seq)
        mla_o = _attn_call(q, k, v, batch=batch, seq=seq)
        gdn_o = _gdn_call(gq, gk, gkb, gvb, g_e, gate, r(jnp.tile(gdn_norm_g[l], N_GDN_HEADS)), ones_bd,
                          batch=batch, seq=seq)
        xt = _ffn_call(xt, r(ffn2_pre_g[l]), bf(ffn2_w_gate[l]), bf(ffn2_w_up[l]), bf(ffn2_w_down[l]),
                       r(ffn2_post_g[l]),
                       mix=(mla_o, gdn_o, r(mla_out_g[l]), bf(w_out[l]), r(mix_post_g[l])))
    return xt.reshape(batch, seq, d)
```

```python
import functools

import jax
import jax.numpy as jnp
import numpy as np
from jax import lax
from jax.experimental import pallas as pl
from jax.experimental.pallas import tpu as pltpu

F32 = jnp.float32
BF16 = jnp.bfloat16

D_MODEL = 1024
N_MLA_HEADS = 8
MLA_Q_RANK = 256
MLA_KV_RANK = 128
MLA_NOPE_DIM = 64
MLA_ROPE_DIM = 32
MLA_V_DIM = 64
ROPE_THETA = 10000.0
N_GDN_HEADS = 8
GDN_HEAD_DIM = 64
GDN_CONV = 4
GDN_CHUNK = 64
MLA_W = N_MLA_HEADS * MLA_V_DIM
GDN_W = N_GDN_HEADS * GDN_HEAD_DIM
D_FF = 2816
EPS = 1e-6

LANES = 128
POS_PACK = LANES // (MLA_ROPE_DIM // 2)
ATTN_BK = 256
HEAD_PAD = 128
NEG_BIG = -1e30
LOG2_E = 1.4426950408889634
CONV_HALO = 8
VMEM_LIMIT = 56 * 1024 * 1024

_PROJ_BLOCKS = (("cq", MLA_Q_RANK), ("ckv", MLA_KV_RANK), ("kpe", LANES), ("kper", LANES),
                ("qkv", 3 * GDN_W), ("gate", GDN_W), ("a", GDN_W), ("b", GDN_W))
_PROJ_OFF = {}
_off = 0
for _name, _width in _PROJ_BLOCKS:
    _PROJ_OFF[_name] = (_off, _off + _width)
    _off += _width
PROJ_COLS = _off


def _rms(x, g):
    return x * lax.rsqrt(jnp.mean(x * x, axis=-1, keepdims=True) + EPS) * g


def _dot(a, b):
    return jnp.dot(a, b, preferred_element_type=F32)


def _const_spec(shape):
    nd = len(shape)
    return pl.BlockSpec(shape, lambda *_: (0,) * nd, pipeline_mode=pl.Buffered(1))


def _swiglu_residual(x, pre_g, wg_ref, wu_ref, wd_ref, post_g, h_ref, fc):
    xn = _rms(x, pre_g).astype(BF16)
    for c in range(wg_ref.shape[1] // fc):
        sl = slice(c * fc, (c + 1) * fc)
        g = _dot(xn, wg_ref[:, sl])
        u = _dot(xn, wu_ref[:, sl])
        h_ref[:, sl] = (g * jax.nn.sigmoid(g) * u).astype(BF16)
    y = _dot(h_ref[...], wd_ref[...])
    return x + 0.5 * _rms(y, post_g)


def _ffn_kernel(x_ref, pre_g_ref, wg_ref, wu_ref, wd_ref, post_g_ref, o_ref, h_ref, *, fc):
    o_ref[...] = _swiglu_residual(x_ref[...], pre_g_ref[...], wg_ref, wu_ref, wd_ref,
                                  post_g_ref[...], h_ref, fc)


def _ffn_mix_kernel(x_ref, mla_ref, gdn_ref, mla_g_ref, wo_ref, mix_g_ref,
                    pre_g_ref, wg_ref, wu_ref, wd_ref, post_g_ref, o_ref, h_ref, *, fc):
    mla = _rms(mla_ref[...], mla_g_ref[...]).astype(BF16)
    mixed = _dot(mla, wo_ref[:MLA_W, :]) + _dot(gdn_ref[...], wo_ref[MLA_W:, :])
    x2 = x_ref[...] + _rms(mixed, mix_g_ref[...])
    o_ref[...] = _swiglu_residual(x2, pre_g_ref[...], wg_ref, wu_ref, wd_ref,
                                  post_g_ref[...], h_ref, fc)


def _ffn_call(x, pre_g, wg, wu, wd, post_g, mix=None, *, tm=512, fc=256):
    n, d = x.shape
    f = wg.shape[1]
    row = lambda w: pl.BlockSpec((tm, w), lambda i: (i, 0))
    ffn_specs = [_const_spec((1, d)), _const_spec((d, f)), _const_spec((d, f)),
                 _const_spec((f, d)), _const_spec((1, d))]
    ffn_args = [pre_g, wg, wu, wd, post_g]
    if mix is None:
        body, in_specs, args = _ffn_kernel, [row(d)] + ffn_specs, [x] + ffn_args
    else:
        mla_o, gdn_o, mla_g, wo, mix_g = mix
        body = _ffn_mix_kernel
        in_specs = [row(d), row(MLA_W), row(GDN_W), _const_spec((1, MLA_W)),
                    _const_spec(wo.shape), _const_spec((1, d))] + ffn_specs
        args = [x, mla_o, gdn_o, mla_g, wo, mix_g] + ffn_args
    return pl.pallas_call(
        functools.partial(body, fc=fc),
        grid=(n // tm,),
        in_specs=in_specs,
        out_specs=row(d),
        out_shape=jax.ShapeDtypeStruct((n, d), F32),
        scratch_shapes=[pltpu.VMEM((tm, f), BF16)],
        compiler_params=pltpu.CompilerParams(dimension_semantics=("arbitrary",),
                                             vmem_limit_bytes=VMEM_LIMIT),
        name="ffn_mix" if mix is not None else "ffn",
    )(*args)


def _proj_kernel(x_ref, pos_ref, pre_g_ref, w_ref, qg_ref, wq1_ref, wq2_ref, kvg_ref, wk_ref, wv_ref,
                 freq_ref, spread_ref, nonrope_ref, vones_ref, convw_ref, alog_ref, dtb_ref, ones_ref,
                 q_ref, k_ref, v_ref, gq_ref, gk_ref, gkb_ref, gvb_ref, g_ref, gate_ref,
                 xbuf_ref, *, tiles_per_seq):
    hn = _rms(x_ref[...], pre_g_ref[...]).astype(BF16)

    def blk(name):
        lo, hi = _PROJ_OFF[name]
        return _dot(hn, w_ref[:, lo:hi])

    tm = x_ref.shape[0]
    ang = pos_ref[...].astype(F32) * freq_ref[...]
    tok = lax.broadcasted_iota(jnp.int32, (tm, LANES), 0) % POS_PACK
    own = lax.broadcasted_iota(jnp.int32, (tm, LANES), 1) // (MLA_ROPE_DIM // 2) == tok

    def to_rope_lanes(packed):
        per_tok = jnp.broadcast_to(packed[:, None, :], (tm // POS_PACK, POS_PACK, LANES)).reshape(tm, LANES)
        per_tok = jnp.where(own, per_tok, 0.0)
        hi = per_tok.astype(BF16)
        lo = (per_tok - hi.astype(F32)).astype(BF16)
        return _dot(hi, spread_ref[...]) + _dot(lo, spread_ref[...])

    sin = to_rope_lanes(jnp.sin(ang))
    cos = to_rope_lanes(jnp.cos(ang)) + nonrope_ref[...]
    scale = (MLA_NOPE_DIM + MLA_ROPE_DIM) ** -0.5 * LOG2_E

    cq =_rms(blk("cq"), qg_ref[...]).astype(BF16)
    reps = (1, N_MLA_HEADS)
    q = _dot(cq, wq1_ref[...]) * jnp.tile(cos * scale, reps) + _dot(cq, wq2_ref[...]) * jnp.tile(sin * scale, reps)
    q_ref[...] = q.astype(BF16)

    ckv = _rms(blk("ckv"), kvg_ref[...]).astype(BF16)
    kpe = blk("kpe") * cos + blk("kper") * sin
    k_ref[...] = (_dot(ckv, wk_ref[...]) + jnp.tile(kpe, reps)).astype(BF16)
    vt = (lax.dot_general(wv_ref[...], ckv, (((1,), (1,)), ((), ())), preferred_element_type=F32)
          + vones_ref[...]).astype(BF16)
    for kb in range(tm // ATTN_BK):
        v_ref[kb] = vt[:, kb * ATTN_BK:(kb + 1) * ATTN_BK]

    w = GDN_W

    @pl.when(pl.program_id(0) % tiles_per_seq == 0)
    def _():
        xbuf_ref[0:CONV_HALO, :] = jnp.zeros((CONV_HALO, 3 * w), F32)

    xbuf_ref[CONV_HALO:CONV_HALO + tm, :] = blk("qkv")
    xe = xbuf_ref[...]
    y = xe[CONV_HALO:, :] * convw_ref[GDN_CONV - 1:GDN_CONV, :]
    for back in range(1, GDN_CONV):
        shifted = pltpu.roll(xe, back, axis=0)[CONV_HALO:, :]
        y = y + shifted * convw_ref[GDN_CONV - 1 - back:GDN_CONV - back, :]
    xbuf_ref[0:CONV_HALO, :] = xe[tm:, :]
    y = y * jax.nn.sigmoid(y)

    def head_sumsq(x):
        return _dot((x * x).astype(BF16), ones_ref[...])

    gq, gk, gv = y[:, 0:w], y[:, w:2 * w], y[:, 2 * w:3 * w]
    gq_ref[...] = (gq * lax.rsqrt(head_sumsq(gq) + EPS) * (GDN_HEAD_DIM ** -0.5)).astype(BF16)
    gk = gk * lax.rsqrt(head_sumsq(gk) + EPS)
    gk_ref[...] = gk.astype(BF16)
    beta = jax.nn.sigmoid(blk("b"))
    gkb_ref[...] = (gk * beta).astype(BF16)
    gvb_ref[...] = (gv * beta).astype(BF16)
    sp_in = blk("a") + dtb_ref[...]
    softplus = jnp.maximum(sp_in, 0.0) + jnp.log1p(jnp.exp(-jnp.abs(sp_in)))
    g_ref[...] = -jnp.exp(alog_ref[...]) * softplus
    gate = blk("gate")
    gate_ref[...] = gate * jax.nn.sigmoid(gate)


def _proj_call(x1, pos, consts, *, seq, tm=512):
    n, d = x1.shape
    row = lambda w: pl.BlockSpec((tm, w), lambda i: (i, 0))
    hw = N_MLA_HEADS * HEAD_PAD
    outs = [(hw, BF16), (hw, BF16),
            (GDN_W, BF16), (GDN_W, BF16), (GDN_W, BF16), (GDN_W, BF16), (GDN_W, F32), (GDN_W, F32)]
    vt_spec = pl.BlockSpec((tm // ATTN_BK, hw, ATTN_BK), lambda i: (i, 0, 0))
    vt_shape = jax.ShapeDtypeStruct((n // ATTN_BK, hw, ATTN_BK), BF16)
    return pl.pallas_call(
        functools.partial(_proj_kernel, tiles_per_seq=seq // tm),
        grid=(n // tm,),
        in_specs=([row(d), pl.BlockSpec((tm // POS_PACK, LANES), lambda i: (i, 0))]
                  + [_const_spec(c.shape) for c in consts]),
        out_specs=[row(w) for w, _ in outs[:2]] + [vt_spec] + [row(w) for w, _ in outs[2:]],
        out_shape=([jax.ShapeDtypeStruct((n, w), dt) for w, dt in outs[:2]] + [vt_shape]
                   + [jax.ShapeDtypeStruct((n, w), dt) for w, dt in outs[2:]]),
        scratch_shapes=[pltpu.VMEM((tm + CONV_HALO, 3 * GDN_W), F32)],
        compiler_params=pltpu.CompilerParams(dimension_semantics=("arbitrary",),
                                             vmem_limit_bytes=VMEM_LIMIT),
        name="proj",
    )(x1, pos, *consts)


def _attn_kernel(q_ref, k_ref, vt_ref, o_ref, m_ref, acc_ref, s_ref, *, seq, bq):
    bk = ATTN_BK
    heads = [slice(hh * HEAD_PAD, (hh + 1) * HEAD_PAD) for hh in range(2)]

    def kv_rows(j):
        return pl.ds(pl.multiple_of(j * bk, bk), bk)

    def scores(qs, j, buf, q_cols=slice(None)):
        for hh, sl in enumerate(heads):
            s_ref[buf, hh, :, q_cols] = lax.dot_general(k_ref[kv_rows(j), sl], qs[hh][q_cols],
                                                        (((1,), (1,)), ((), ())), preferred_element_type=F32)

    def accumulate(j, buf, q_cols=slice(None), mask=None):
        for hh, sl in enumerate(heads):
            s = s_ref[buf, hh, :, q_cols]
            if mask is not None:
                s = jnp.where(mask, s, NEG_BIG)
            m_old = m_ref[hh, :, q_cols]
            m_new = jnp.maximum(m_old, jnp.max(s, axis=0, keepdims=True))
            p = jnp.exp2(s - m_new)
            acc_ref[hh, :, q_cols] = (jnp.exp2(m_old - m_new) * acc_ref[hh, :, q_cols]
                                      + _dot(vt_ref[j, sl, :], p.astype(BF16)))
            m_ref[hh, :, q_cols] = m_new

    def q_block(qi, carry):
        rows = pl.ds(pl.multiple_of(qi * bq, bq), bq)
        qs = [q_ref[rows, sl] for sl in heads]
        m_ref[...] = jnp.full(m_ref.shape, NEG_BIG, F32)
        acc_ref[...] = jnp.zeros(acc_ref.shape, F32)
        scores(qs, 0, 0)

        def body(t, c):
            scores(qs, 2 * t + 1, 1)
            accumulate(2 * t, 0)
            scores(qs, 2 * t + 2, 0)
            accumulate(2 * t + 1, 1)
            return c

        lax.fori_loop(0, qi, body, 0)
        scores(qs, 2 * qi + 1, 1, q_cols=slice(bk, bq))
        accumulate(2 * qi, 0, mask=(lax.broadcasted_iota(jnp.int32, (bk, bq), 0)
                                    <= lax.broadcasted_iota(jnp.int32, (bk, bq), 1)))
        accumulate(2 * qi + 1, 1, q_cols=slice(bk, bq),
                   mask=(lax.broadcasted_iota(jnp.int32, (bk, bk), 0)
                         <= lax.broadcasted_iota(jnp.int32, (bk, bk), 1)))
        a0, a1 = acc_ref[0], acc_ref[1]
        num = jnp.concatenate([a0[:MLA_V_DIM], a1[MLA_V_DIM:]], axis=0)
        den = jnp.concatenate([a0[MLA_V_DIM:], a1[:MLA_V_DIM]], axis=0)
        o_ref[rows, :] = (num / den).T
        return carry

    lax.fori_loop(0, seq // bq, q_block, 0)


def _attn_call(q, k, vt, *, batch, seq):
    bq = 2 * ATTN_BK
    pairs = N_MLA_HEADS // 2
    in_spec = pl.BlockSpec((seq, 2 * HEAD_PAD), lambda b, p: (b, p))
    vt_spec = pl.BlockSpec((seq // ATTN_BK, 2 * HEAD_PAD, ATTN_BK), lambda b, p: (b, p, 0))
    return pl.pallas_call(
        functools.partial(_attn_kernel, seq=seq, bq=bq),
        grid=(batch, pairs),
        in_specs=[in_spec, in_spec, vt_spec],
        out_specs=pl.BlockSpec((seq, LANES), lambda b, p: (b, p)),
        out_shape=jax.ShapeDtypeStruct((batch * seq, MLA_W), F32),
        scratch_shapes=[pltpu.VMEM((2, 1, bq), F32), pltpu.VMEM((2, HEAD_PAD, bq), F32),
                        pltpu.VMEM((2, 2, ATTN_BK, bq), F32)],
        compiler_params=pltpu.CompilerParams(dimension_semantics=("arbitrary", "arbitrary"),
                                             vmem_limit_bytes=VMEM_LIMIT),
        name="attn",
    )(q, k, vt)


GROUP_HEADS = LANES // GDN_HEAD_DIM
GROUP_LANES = GROUP_HEADS * GDN_HEAD_DIM
GROUP_SLICES = tuple(slice(i * GROUP_LANES, (i + 1) * GROUP_LANES) for i in range(GDN_W // GROUP_LANES))


def _head_in_group(shape, dtype=jnp.int32):
    lane = lax.broadcasted_iota(jnp.int32, shape, 1) % GROUP_LANES // GDN_HEAD_DIM
    return lane.astype(F32).astype(dtype)


def _group_blockdiag(x_grp):
    x_grp = x_grp.astype(BF16)
    head = _head_in_group(x_grp.shape, BF16)
    zero = jnp.zeros_like(x_grp)
    return jnp.concatenate([jnp.where(head == h, x_grp, zero) for h in range(GROUP_HEADS)], axis=0)


def _heads_matmul(a_all, b_all):
    return jnp.concatenate([_dot(a_all[:, sl].astype(BF16), _group_blockdiag(b_all[:, sl]))
                            for sl in GROUP_SLICES], axis=1)


def _heads_gram(a_all, b_all):
    return jnp.concatenate([lax.dot_general(a_all[:, sl].astype(BF16), _group_blockdiag(b_all[:, sl]),
                                            (((1,), (1,)), ((), ())), preferred_element_type=F32)
                            for sl in GROUP_SLICES], axis=1)


def _heads_outer(a_all, b_all):
    outs = []
    head = _head_in_group((GDN_HEAD_DIM, GROUP_LANES))
    for sl in GROUP_SLICES:
        full = lax.dot_general(a_all[:, sl].astype(BF16), b_all[:, sl].astype(BF16),
                               (((0,), (0,)), ((), ())), preferred_element_type=F32)
        diag = full[:GDN_HEAD_DIM]
        for h in range(1, GROUP_HEADS):
            diag = jnp.where(head == h, full[h * GDN_HEAD_DIM:(h + 1) * GDN_HEAD_DIM], diag)
        outs.append(diag)
    return jnp.concatenate(outs, axis=1)


def _unit_lower_inverse(l_all, row, col):
    eye = jnp.where(row == col, 1.0, 0.0).astype(F32)
    zero = jnp.zeros_like(row, dtype=F32)
    same16 = (row // 16) == (col // 16)
    same32 = (row // 32) == (col // 32)
    mm = lambda xs, ys: [_heads_matmul(x, y) for x, y in zip(xs, ys)]
    power = [jnp.where(same16, -l, zero) for l in l_all]
    inv = [eye + m for m in power]
    for _ in range(3):
        power = mm(power, power)
        inv = [i + pi for i, pi in zip(inv, mm(power, inv))]
    for pick in (same32 & ~same16, ~same32):
        off = [jnp.where(pick, l, zero) for l in l_all]
        inv = [i - x for i, x in zip(inv, mm(inv, mm(off, inv)))]
    return inv


def _gdn_kernel(q_ref, k_ref, kb_ref, vb_ref, g_ref, gate_ref, ng_ref, ones_ref, o_ref, s_ref, *, tm):
    c = GDN_CHUNK
    w = GDN_W

    @pl.when(pl.program_id(1) == 0)
    def _():
        s_ref[...] = jnp.zeros(s_ref.shape, F32)

    row = lax.broadcasted_iota(jnp.int32, (c, w), 0)
    col = lax.broadcasted_iota(jnp.int32, (c, w), 1) % GDN_HEAD_DIM
    tril = jnp.where(lax.broadcasted_iota(jnp.int32, (c, c), 1) <= lax.broadcasted_iota(jnp.int32, (c, c), 0),
                     1.0, 0.0).astype(BF16)
    zero = jnp.zeros((c, w), F32)

    def tril_dot(x):
        hi = x.astype(BF16)
        lo = (x - hi.astype(F32)).astype(BF16)
        return _dot(tril, hi) + _dot(tril, lo)

    chunks = [slice(ci * c, (ci + 1) * c) for ci in range(tm // c)]
    qs, ks, kbs = ([ref[rs, :] for rs in chunks] for ref in (q_ref, k_ref, kb_ref))
    gs = [g_ref[rs, :] for rs in chunks]
    sums = [tril_dot(jnp.concatenate([g, jnp.where(col < row, g, zero)], axis=1)) for g in gs]
    gcs = [sm[:, :w] for sm in sums]
    decays = [jnp.exp(jnp.where(col <= row, sm[:, w:], NEG_BIG)) for sm in sums]
    egcs = [jnp.exp(gc) for gc in gcs]
    grams = [_heads_gram(jnp.concatenate([kb, qc], axis=0), kc) for kb, qc, kc in zip(kbs, qs, ks)]
    t_alls = _unit_lower_inverse([jnp.where(col < row, gr[:c] * dc, zero) for gr, dc in zip(grams, decays)],
                                 row, col)
    attns = [gr[c:] * dc for gr, dc in zip(grams, decays)]
    w_alls = [_heads_matmul(t, kb.astype(F32) * e) for t, kb, e in zip(t_alls, kbs, egcs)]
    u_alls = [_heads_matmul(t, vb_ref[rs, :]) for t, rs in zip(t_alls, chunks)]
    q_decs = [qc.astype(F32) * e for qc, e in zip(qs, egcs)]
    k_decs = [kc.astype(F32) * jnp.exp(gc[c - 1:c, :] - gc) for kc, gc in zip(ks, gcs)]
    s_decays = [jnp.exp(gc[c - 1:c, :]) for gc in gcs]

    s_all = s_ref[...]
    outs = []
    for ci in range(tm // c):
        ws_qs = _heads_matmul(jnp.concatenate([w_alls[ci], q_decs[ci]], axis=0), s_all)
        v_new = u_alls[ci] - ws_qs[:c]
        outs.append(ws_qs[c:] + _heads_matmul(attns[ci], v_new))
        s_all = s_all * s_decays[ci] + _heads_outer(k_decs[ci], v_new)
    s_ref[...] = s_all

    o = jnp.concatenate(outs, axis=0)
    var = _dot((o * o).astype(BF16), ones_ref[...]) * (1.0 / GDN_HEAD_DIM)
    o_ref[...] = (o * lax.rsqrt(var + EPS) * ng_ref[...] * gate_ref[...]).astype(BF16)


def _gdn_call(gq, gk, gkb, gvb, g_e, gate, ng_e, ones_bd, *, batch, seq, tm=512):
    nt = seq // tm
    row = pl.BlockSpec((tm, GDN_W), lambda b, j: (b * nt + j, 0))
    consts = [ng_e, ones_bd]
    return pl.pallas_call(
        functools.partial(_gdn_kernel, tm=tm),
        grid=(batch, nt),
        in_specs=[row] * 6 + [_const_spec(x.shape) for x in consts],
        out_specs=row,
        out_shape=jax.ShapeDtypeStruct((batch * seq, GDN_W), BF16),
        scratch_shapes=[pltpu.VMEM((GDN_HEAD_DIM, GDN_W), F32)],
        compiler_params=pltpu.CompilerParams(dimension_semantics=("arbitrary", "arbitrary"),
                                             vmem_limit_bytes=VMEM_LIMIT),
        name="gdn",
    )(gq, gk, gkb, gvb, g_e, gate, *consts)


def _rotate_half_cols(w):
    half = MLA_ROPE_DIM // 2
    return jnp.concatenate([-w[:, half:], w[:, :half]], axis=1)


def _prep_proj_weights(w_in, w_uq, w_ukv):
    d = w_in.shape[0]
    sizes = (MLA_Q_RANK, MLA_KV_RANK, MLA_ROPE_DIM, 3 * GDN_W, N_GDN_HEADS, N_GDN_HEADS, GDN_W)
    cuts = np.cumsum(sizes)[:-1]
    w_cq, w_ckv, w_kpe, w_qkv, w_a, w_b, w_gate = jnp.split(w_in, cuts, axis=1)

    def rope_block(wpe):
        z_lo = jnp.zeros((d, MLA_NOPE_DIM), F32)
        z_hi = jnp.zeros((d, HEAD_PAD - MLA_NOPE_DIM - MLA_ROPE_DIM), F32)
        return jnp.concatenate([z_lo, wpe, z_hi], axis=1)

    w_all = jnp.concatenate([
        w_cq, w_ckv, rope_block(w_kpe), rope_block(_rotate_half_cols(w_kpe)), w_qkv, w_gate,
        jnp.repeat(w_a, GDN_HEAD_DIM, axis=1), jnp.repeat(w_b, GDN_HEAD_DIM, axis=1)], axis=1).astype(BF16)

    qh = w_uq.reshape(MLA_Q_RANK, N_MLA_HEADS, MLA_NOPE_DIM + MLA_ROPE_DIM)
    q_nope, q_pe = qh[..., :MLA_NOPE_DIM], qh[..., MLA_NOPE_DIM:]
    q_pe_rot = jnp.concatenate([-q_pe[..., MLA_ROPE_DIM // 2:], q_pe[..., :MLA_ROPE_DIM // 2]], axis=-1)
    pad = jnp.zeros((MLA_Q_RANK, N_MLA_HEADS, HEAD_PAD - MLA_NOPE_DIM - MLA_ROPE_DIM), F32)
    wq1 = jnp.concatenate([q_nope, q_pe, pad], axis=-1).reshape(MLA_Q_RANK, -1).astype(BF16)
    wq2 = jnp.concatenate([jnp.zeros_like(q_nope), q_pe_rot, pad], axis=-1).reshape(MLA_Q_RANK, -1).astype(BF16)

    kvh = w_ukv.reshape(MLA_KV_RANK, N_MLA_HEADS, MLA_NOPE_DIM + MLA_V_DIM)
    k_nope, v = kvh[..., :MLA_NOPE_DIM], kvh[..., MLA_NOPE_DIM:]
    kpad = jnp.zeros((MLA_KV_RANK, N_MLA_HEADS, HEAD_PAD - MLA_NOPE_DIM), F32)
    wk = jnp.concatenate([k_nope, kpad], axis=-1).reshape(MLA_KV_RANK, -1).astype(BF16)
    vp = v.reshape(MLA_KV_RANK, N_MLA_HEADS // 2, 2, MLA_V_DIM)
    vz = jnp.zeros_like(vp[:, :, 0])
    wv = jnp.concatenate([vp[:, :, 0], vz, vz, vp[:, :, 1]], axis=-1).reshape(MLA_KV_RANK, -1).T.astype(BF16)
    return w_all, wq1, wq2, wk, wv


def _value_ones_rows():
    pair = jnp.concatenate([jnp.zeros((MLA_V_DIM,), F32), jnp.ones((2 * MLA_V_DIM,), F32),
                            jnp.zeros((MLA_V_DIM,), F32)])
    return jnp.tile(pair, N_MLA_HEADS // 2)[:, None]


def _rope_constants():
    half = MLA_ROPE_DIM // 2
    freqs = ROPE_THETA ** (-jnp.arange(half, dtype=F32) / half)
    freq_packed = jnp.tile(freqs, LANES // half)[None, :]
    j = jnp.arange(LANES) % half
    lane = jnp.arange(LANES)
    spread = ((lane[None, :] == MLA_NOPE_DIM + j[:, None])
              | (lane[None, :] == MLA_NOPE_DIM + half + j[:, None])).astype(BF16)
    nonrope = ((lane < MLA_NOPE_DIM) | (lane >= MLA_NOPE_DIM + MLA_ROPE_DIM)).astype(F32)[None, :]
    return freq_packed, spread, nonrope


def kernel(x, positions, ffn1_pre_g, ffn1_w_gate, ffn1_w_up, ffn1_w_down, ffn1_post_g, mix_pre_g, w_in, mla_q_norm_g, mla_w_uq, mla_kv_norm_g, mla_w_ukv, mla_out_g, gdn_conv_w, gdn_a_log, gdn_dt_bias, gdn_norm_g, w_out, mix_post_g, ffn2_pre_g, ffn2_w_gate, ffn2_w_up, ffn2_w_down, ffn2_post_g):
    batch, seq, d = x.shape
    n = batch * seq
    xt = x.reshape(n, d)
    pos = jnp.repeat(positions.reshape(n), MLA_ROPE_DIM // 2).reshape(n // POS_PACK, LANES)
    freq, spread, nonrope = _rope_constants()
    head_ids = jnp.arange(GDN_W) // GDN_HEAD_DIM
    ones_bd = (head_ids[:, None] == head_ids[None, :]).astype(BF16)
    bf = lambda w: w.astype(BF16)
    r = lambda g: g[None, :]

    for l in range(ffn1_pre_g.shape[0]):
        xt = _ffn_call(xt, r(ffn1_pre_g[l]), bf(ffn1_w_gate[l]), bf(ffn1_w_up[l]), bf(ffn1_w_down[l]),
                       r(ffn1_post_g[l]))
        w_all, wq1, wq2, wk, wv = _prep_proj_weights(w_in[l], mla_w_uq[l], mla_w_ukv[l])
        proj_consts = [r(mix_pre_g[l]), w_all, r(mla_q_norm_g[l]), wq1, wq2, r(mla_kv_norm_g[l]), wk, wv,
                       freq, spread, nonrope, _value_ones_rows(), gdn_conv_w[l],
                       r(jnp.repeat(gdn_a_log[l].astype(F32), GDN_HEAD_DIM)),
                       r(jnp.repeat(gdn_dt_bias[l].astype(F32), GDN_HEAD_DIM)), ones_bd]
        q, k, v, gq, gk, gkb, gvb, g_e, gate = _proj_call(xt, pos, proj_consts, seq=seq)
        mla_o = _attn_call(q, k, v, batch=batch, seq=seq)
        gdn_o = _gdn_call(gq, gk, gkb, gvb, g_e, gate, r(jnp.tile(gdn_norm_g[l], N_GDN_HEADS)), ones_bd,
                          batch=batch, seq=seq)
        xt = _ffn_call(xt, r(ffn2_pre_g[l]), bf(ffn2_w_gate[l]), bf(ffn2_w_up[l]), bf(ffn2_w_down[l]),
                       r(ffn2_post_g[l]),
                       mix=(mla_o, gdn_o, r(mla_out_g[l]), bf(w_out[l]), r(mix_post_g[l])))
    return xt.reshape(batch, seq, d)
```

```python
import functools

import jax
import jax.numpy as jnp
import numpy as np
from jax import lax
from jax.experimental import pallas as pl
from jax.experimental.pallas import tpu as pltpu

F32 = jnp.float32
BF16 = jnp.bfloat16

D_MODEL = 1024
N_MLA_HEADS = 8
MLA_Q_RANK = 256
MLA_KV_RANK = 128
MLA_NOPE_DIM = 64
MLA_ROPE_DIM = 32
MLA_V_DIM = 64
ROPE_THETA = 10000.0
N_GDN_HEADS = 8
GDN_HEAD_DIM = 64
GDN_CONV = 4
GDN_CHUNK = 64
MLA_W = N_MLA_HEADS * MLA_V_DIM
GDN_W = N_GDN_HEADS * GDN_HEAD_DIM
D_FF = 2816
EPS = 1e-6

LANES = 128
POS_PACK = LANES // (MLA_ROPE_DIM // 2)
ATTN_BK = 256
HEAD_PAD = 128
NEG_BIG = -1e30
LOG2_E = 1.4426950408889634
CONV_HALO = 8
VMEM_LIMIT = 56 * 1024 * 1024

_PROJ_BLOCKS = (("cq", MLA_Q_RANK), ("ckv", MLA_KV_RANK), ("kpe", LANES), ("kper", LANES),
                ("qkv", 3 * GDN_W), ("gate", GDN_W), ("a", GDN_W), ("b", GDN_W))
_PROJ_OFF = {}
_off = 0
for _name, _width in _PROJ_BLOCKS:
    _PROJ_OFF[_name] = (_off, _off + _width)
    _off += _width
PROJ_COLS = _off


def _rms(x, g):
    return x * lax.rsqrt(jnp.mean(x * x, axis=-1, keepdims=True) + EPS) * g


def _dot(a, b):
    return jnp.dot(a, b, preferred_element_type=F32)


def _const_spec(shape):
    nd = len(shape)
    return pl.BlockSpec(shape, lambda *_: (0,) * nd, pipeline_mode=pl.Buffered(1))


def _swiglu_residual(x, pre_g, wg_ref, wu_ref, wd_ref, post_g, h_ref, fc):
    xn = _rms(x, pre_g).astype(BF16)
    for c in range(wg_ref.shape[1] // fc):
        sl = slice(c * fc, (c + 1) * fc)
        g = _dot(xn, wg_ref[:, sl])
        u = _dot(xn, wu_ref[:, sl])
        h_ref[:, sl] = (g * jax.nn.sigmoid(g) * u).astype(BF16)
    y = _dot(h_ref[...], wd_ref[...])
    return x + 0.5 * _rms(y, post_g)


def _ffn_kernel(x_ref, pre_g_ref, wg_ref, wu_ref, wd_ref, post_g_ref, o_ref, h_ref, *, fc):
    o_ref[...] = _swiglu_residual(x_ref[...], pre_g_ref[...], wg_ref, wu_ref, wd_ref,
                                  post_g_ref[...], h_ref, fc)


def _ffn_mix_kernel(x_ref, mla_ref, gdn_ref, mla_g_ref, wo_ref, mix_g_ref,
                    pre_g_ref, wg_ref, wu_ref, wd_ref, post_g_ref, o_ref, h_ref, *, fc):
    mla = _rms(mla_ref[...], mla_g_ref[...]).astype(BF16)
    mixed = _dot(mla, wo_ref[:MLA_W, :]) + _dot(gdn_ref[...], wo_ref[MLA_W:, :])
    x2 = x_ref[...] + _rms(mixed, mix_g_ref[...])
    o_ref[...] = _swiglu_residual(x2, pre_g_ref[...], wg_ref, wu_ref, wd_ref,
                                  post_g_ref[...], h_ref, fc)


def _ffn_call(x, pre_g, wg, wu, wd, post_g, mix=None, *, tm=1024, fc=256):
    n, d = x.shape
    f = wg.shape[1]
    row = lambda w: pl.BlockSpec((tm, w), lambda i: (i, 0))
    ffn_specs = [_const_spec((1, d)), _const_spec((d, f)), _const_spec((d, f)),
                 _const_spec((f, d)), _const_spec((1, d))]
    ffn_args = [pre_g, wg, wu, wd, post_g]
    if mix is None:
        body, in_specs, args = _ffn_kernel, [row(d)] + ffn_specs, [x] + ffn_args
    else:
        mla_o, gdn_o, mla_g, wo, mix_g = mix
        body = _ffn_mix_kernel
        in_specs = [row(d), row(MLA_W), row(GDN_W), _const_spec((1, MLA_W)),
                    _const_spec(wo.shape), _const_spec((1, d))] + ffn_specs
        args = [x, mla_o, gdn_o, mla_g, wo, mix_g] + ffn_args
    return pl.pallas_call(
        functools.partial(body, fc=fc),
        grid=(n // tm,),
        in_specs=in_specs,
        out_specs=row(d),
        out_shape=jax.ShapeDtypeStruct((n, d), F32),
        scratch_shapes=[pltpu.VMEM((tm, f), BF16)],
        compiler_params=pltpu.CompilerParams(dimension_semantics=("arbitrary",),
                                             vmem_limit_bytes=VMEM_LIMIT),
        name="ffn_mix" if mix is not None else "ffn",
    )(*args)


def _proj_kernel(x_ref, pos_ref, pre_g_ref, w_ref, qg_ref, wq1_ref, wq2_ref, kvg_ref, wk_ref, wv_ref,
                 freq_ref, spread_ref, nonrope_ref, vones_ref, convw_ref, alog_ref, dtb_ref, ones_ref,
                 q_ref, k_ref, v_ref, gq_ref, gk_ref, gkb_ref, gvb_ref, g_ref, gate_ref,
                 xbuf_ref, *, tiles_per_seq):
    hn = _rms(x_ref[...], pre_g_ref[...]).astype(BF16)

    def blk(name):
        lo, hi = _PROJ_OFF[name]
        return _dot(hn, w_ref[:, lo:hi])

    w = GDN_W
    tm = x_ref.shape[0]

    @pl.when(pl.program_id(0) % tiles_per_seq == 0)
    def _():
        xbuf_ref[0:CONV_HALO, :] = jnp.zeros((CONV_HALO, 3 * w), F32)

    xbuf_ref[CONV_HALO:CONV_HALO + tm, :] = blk("qkv")
    xe = xbuf_ref[...]
    y = xe[CONV_HALO:, :] * convw_ref[GDN_CONV - 1:GDN_CONV, :]
    for back in range(1, GDN_CONV):
        shifted = pltpu.roll(xe, back, axis=0)[CONV_HALO:, :]
        y = y + shifted * convw_ref[GDN_CONV - 1 - back:GDN_CONV - back, :]
    xbuf_ref[0:CONV_HALO, :] = xe[tm:, :]
    y = y * jax.nn.sigmoid(y)

    def head_sumsq(x):
        return _dot((x * x).astype(BF16), ones_ref[...])

    gq, gk, gv = y[:, 0:w], y[:, w:2 * w], y[:, 2 * w:3 * w]
    gq_ref[...] = (gq * lax.rsqrt(head_sumsq(gq) + EPS) * (GDN_HEAD_DIM ** -0.5)).astype(BF16)
    gk = gk * lax.rsqrt(head_sumsq(gk) + EPS)
    gk_ref[...] = gk.astype(BF16)
    beta = jax.nn.sigmoid(blk("b"))
    gkb_ref[...] = (gk * beta).astype(BF16)
    gvb_ref[...] = (gv * beta).astype(BF16)
    sp_in = blk("a") + dtb_ref[...]
    softplus = jnp.maximum(sp_in, 0.0) + jnp.log1p(jnp.exp(-jnp.abs(sp_in)))
    g_ref[...] = -jnp.exp(alog_ref[...]) * softplus
    gate = blk("gate")
    gate_ref[...] = gate * jax.nn.sigmoid(gate)

    ang = pos_ref[...].astype(F32) * freq_ref[...]
    tok = lax.broadcasted_iota(jnp.int32, (tm, LANES), 0) % POS_PACK
    own = lax.broadcasted_iota(jnp.int32, (tm, LANES), 1) // (MLA_ROPE_DIM // 2) == tok

    def to_rope_lanes(packed):
        per_tok = jnp.broadcast_to(packed[:, None, :], (tm // POS_PACK, POS_PACK, LANES)).reshape(tm, LANES)
        per_tok = jnp.where(own, per_tok, 0.0)
        hi = per_tok.astype(BF16)
        lo = (per_tok - hi.astype(F32)).astype(BF16)
        return _dot(hi, spread_ref[...]) + _dot(lo, spread_ref[...])

    sin = to_rope_lanes(jnp.sin(ang))
    cos = to_rope_lanes(jnp.cos(ang)) + nonrope_ref[...]
    scale = (MLA_NOPE_DIM + MLA_ROPE_DIM) ** -0.5 * LOG2_E

    cq = _rms(blk("cq"), qg_ref[...]).astype(BF16)
    reps = (1, N_MLA_HEADS)
    q = _dot(cq, wq1_ref[...]) * jnp.tile(cos * scale, reps) + _dot(cq, wq2_ref[...]) * jnp.tile(sin * scale, reps)
    q_ref[...] = q.astype(BF16)

    ckv = _rms(blk("ckv"), kvg_ref[...]).astype(BF16)
    kpe = blk("kpe") * cos + blk("kper") * sin
    k_ref[...] = (_dot(ckv, wk_ref[...]) + jnp.tile(kpe, reps)).astype(BF16)
    vt = (lax.dot_general(wv_ref[...], ckv, (((1,), (1,)), ((), ())), preferred_element_type=F32)
          + vones_ref[...]).astype(BF16)
    for kb in range(tm // ATTN_BK):
        v_ref[kb] = vt[:, kb * ATTN_BK:(kb + 1) * ATTN_BK]


def _proj_call(x1, pos, consts, *, seq, tm=512):
    n, d = x1.shape
    row = lambda w: pl.BlockSpec((tm, w), lambda i: (i, 0))
    hw = N_MLA_HEADS * HEAD_PAD
    outs = [(hw, BF16), (hw, BF16),
            (GDN_W, BF16), (GDN_W, BF16), (GDN_W, BF16), (GDN_W, BF16), (GDN_W, F32), (GDN_W, F32)]
    vt_spec = pl.BlockSpec((tm // ATTN_BK, hw, ATTN_BK), lambda i: (i, 0, 0))
    vt_shape = jax.ShapeDtypeStruct((n // ATTN_BK, hw, ATTN_BK), BF16)
    return pl.pallas_call(
        functools.partial(_proj_kernel, tiles_per_seq=seq // tm),
        grid=(n // tm,),
        in_specs=([row(d), pl.BlockSpec((tm // POS_PACK, LANES), lambda i: (i, 0))]
                  + [_const_spec(c.shape) for c in consts]),
        out_specs=[row(w) for w, _ in outs[:2]] + [vt_spec] + [row(w) for w, _ in outs[2:]],
        out_shape=([jax.ShapeDtypeStruct((n, w), dt) for w, dt in outs[:2]] + [vt_shape]
                   + [jax.ShapeDtypeStruct((n, w), dt) for w, dt in outs[2:]]),
        scratch_shapes=[pltpu.VMEM((tm + CONV_HALO, 3 * GDN_W), F32)],
        compiler_params=pltpu.CompilerParams(dimension_semantics=("arbitrary",),
                                             vmem_limit_bytes=VMEM_LIMIT),
        name="proj",
    )(x1, pos, *consts)


def _attn_kernel(q_ref, k_ref, vt_ref, o_ref, m_ref, acc_ref, s_ref, *, seq, bq):
    bk = ATTN_BK
    nk = bq // bk
    heads = [slice(hh * HEAD_PAD, (hh + 1) * HEAD_PAD) for hh in range(2)]

    def kv_rows(j):
        return pl.ds(pl.multiple_of(j * bk, bk), bk)

    def scores(qs, j, buf, q_cols=slice(None)):
        for hh, sl in enumerate(heads):
            s_ref[buf, hh, :, q_cols] = lax.dot_general(k_ref[kv_rows(j), sl], qs[hh][q_cols],
                                                        (((1,), (1,)), ((), ())), preferred_element_type=F32)

    def accumulate(j, buf, q_cols=slice(None), mask=None):
        for hh, sl in enumerate(heads):
            s = s_ref[buf, hh, :, q_cols]
            if mask is not None:
                s = jnp.where(mask, s, NEG_BIG)
            m_old = m_ref[hh, :, q_cols]
            m_new = jnp.maximum(m_old, jnp.max(s, axis=0, keepdims=True))
            p = jnp.exp2(s - m_new)
            acc_ref[hh, :, q_cols] = (jnp.exp2(m_old - m_new) * acc_ref[hh, :, q_cols]
                                      + _dot(vt_ref[j, sl, :], p.astype(BF16)))
            m_ref[hh, :, q_cols] = m_new

    def q_block(qi, carry):
        rows = pl.ds(pl.multiple_of(qi * bq, bq), bq)
        qs = [q_ref[rows, sl] for sl in heads]
        m_ref[...] = jnp.full(m_ref.shape, NEG_BIG, F32)
        acc_ref[...] = jnp.zeros(acc_ref.shape, F32)
        scores(qs, 0, 0)

        def body(t, c):
            scores(qs, 2 * t + 1, 1)
            accumulate(2 * t, 0)
            scores(qs, 2 * t + 2, 0)
            accumulate(2 * t + 1, 1)
            return c

        lax.fori_loop(0, qi * (nk // 2), body, 0)
        for d in range(nk):
            cols = slice(d * bk, bq)
            if d + 1 < nk:
                scores(qs, nk * qi + d + 1, (d + 1) % 2, q_cols=slice((d + 1) * bk, bq))
            accumulate(nk * qi + d, d % 2, q_cols=cols,
                       mask=(lax.broadcasted_iota(jnp.int32, (bk, bq - d * bk), 0)
                             <= lax.broadcasted_iota(jnp.int32, (bk, bq - d * bk), 1)))
        a0, a1 = acc_ref[0], acc_ref[1]
        num = jnp.concatenate([a0[:MLA_V_DIM], a1[MLA_V_DIM:]], axis=0)
        den = jnp.concatenate([a0[MLA_V_DIM:], a1[:MLA_V_DIM]], axis=0)
        o_ref[rows, :] = (num / den).T
        return carry

    lax.fori_loop(0, seq // bq, q_block, 0)


def _attn_call(q, k, vt, *, batch, seq, bq=1024):
    pairs = N_MLA_HEADS // 2
    in_spec = pl.BlockSpec((seq, 2 * HEAD_PAD), lambda b, p: (b, p))
    vt_spec = pl.BlockSpec((seq // ATTN_BK, 2 * HEAD_PAD, ATTN_BK), lambda b, p: (b, p, 0))
    return pl.pallas_call(
        functools.partial(_attn_kernel, seq=seq, bq=bq),
        grid=(batch, pairs),
        in_specs=[in_spec, in_spec, vt_spec],
        out_specs=pl.BlockSpec((seq, LANES), lambda b, p: (b, p)),
        out_shape=jax.ShapeDtypeStruct((batch * seq, MLA_W), F32),
        scratch_shapes=[pltpu.VMEM((2, 1, bq), F32), pltpu.VMEM((2, HEAD_PAD, bq), F32),
                        pltpu.VMEM((2, 2, ATTN_BK, bq), F32)],
        compiler_params=pltpu.CompilerParams(dimension_semantics=("arbitrary", "arbitrary"),
                                             vmem_limit_bytes=VMEM_LIMIT),
        name="attn",
    )(q, k, vt)


GROUP_HEADS = LANES // GDN_HEAD_DIM
GROUP_LANES = GROUP_HEADS * GDN_HEAD_DIM
GROUP_SLICES = tuple(slice(i * GROUP_LANES, (i + 1) * GROUP_LANES) for i in range(GDN_W // GROUP_LANES))


def _head_in_group(shape, dtype=jnp.int32):
    lane = lax.broadcasted_iota(jnp.int32, shape, 1) % GROUP_LANES // GDN_HEAD_DIM
    return lane.astype(F32).astype(dtype)


def _group_blockdiag(x_grp):
    x_grp = x_grp.astype(BF16)
    head = _head_in_group(x_grp.shape, BF16)
    zero = jnp.zeros_like(x_grp)
    return jnp.concatenate([jnp.where(head == h, x_grp, zero) for h in range(GROUP_HEADS)], axis=0)


def _heads_matmul(a_all, *b_alls):
    outs = [[] for _ in b_alls]
    for sl in GROUP_SLICES:
        rhs = jnp.concatenate([_group_blockdiag(b[:, sl]) for b in b_alls], axis=1)
        prod = _dot(a_all[:, sl].astype(BF16), rhs)
        for k, out in enumerate(outs):
            out.append(prod[:, k * GROUP_LANES:(k + 1) * GROUP_LANES])
    outs = [jnp.concatenate(out, axis=1) for out in outs]
    return outs[0] if len(outs) == 1 else outs


def _heads_gram(a_all, b_all):
    return jnp.concatenate([lax.dot_general(a_all[:, sl].astype(BF16), _group_blockdiag(b_all[:, sl]),
                                            (((1,), (1,)), ((), ())), preferred_element_type=F32)
                            for sl in GROUP_SLICES], axis=1)


def _heads_outer(a_all, *b_alls):
    outs = [[] for _ in b_alls]
    head = _head_in_group((GDN_HEAD_DIM, GROUP_LANES))
    for sl in GROUP_SLICES:
        rhs = jnp.concatenate([b[:, sl].astype(BF16) for b in b_alls], axis=1)
        full = lax.dot_general(a_all[:, sl].astype(BF16), rhs,
                               (((0,), (0,)), ((), ())), preferred_element_type=F32)
        for k, out in enumerate(outs):
            blk = full[:, k * GROUP_LANES:(k + 1) * GROUP_LANES]
            diag = blk[:GDN_HEAD_DIM]
            for h in range(1, GROUP_HEADS):
                diag = jnp.where(head == h, blk[h * GDN_HEAD_DIM:(h + 1) * GDN_HEAD_DIM], diag)
            out.append(diag)
    outs = [jnp.concatenate(out, axis=1) for out in outs]
    return outs[0] if len(outs) == 1 else outs


def _unit_lower_inverse(l_all, row, col):
    eye = jnp.where(row == col, 1.0, 0.0).astype(F32)
    zero = jnp.zeros_like(row, dtype=F32)
    same16 = (row // 16) == (col // 16)
    same32 = (row // 32) == (col // 32)
    mm = lambda xs, ys: [_heads_matmul(x, y) for x, y in zip(xs, ys)]
    m1 = [jnp.where(same16, -l, zero) for l in l_all]
    inv = [eye + m for m in m1]
    power = mm(m1, m1)
    for _ in range(2):
        both = [_heads_matmul(p, i, p) for p, i in zip(power, inv)]
        inv = [i + b[0] for i, b in zip(inv, both)]
        power = [b[1] for b in both]
    inv = [i + pi for i, pi in zip(inv, mm(power, inv))]
    for pick in (same32 & ~same16, ~same32):
        off = [jnp.where(pick, l, zero) for l in l_all]
        inv = [i - x for i, x in zip(inv, mm(inv, mm(off, inv)))]
    return inv


def _gdn_kernel(q_ref, k_ref, kb_ref, vb_ref, g_ref, gate_ref, ng_ref, ones_ref, o_ref, s_ref, *, tm):
    c = GDN_CHUNK
    w = GDN_W

    @pl.when(pl.program_id(1) == 0)
    def _():
        s_ref[...] = jnp.zeros(s_ref.shape, F32)

    row = lax.broadcasted_iota(jnp.int32, (c, w), 0)
    col = lax.broadcasted_iota(jnp.int32, (c, w), 1) % GDN_HEAD_DIM
    tril = jnp.where(lax.broadcasted_iota(jnp.int32, (c, c), 1) <= lax.broadcasted_iota(jnp.int32, (c, c), 0),
                     1.0, 0.0).astype(BF16)
    zero = jnp.zeros((c, w), F32)

    def tril_dot(x):
        hi = x.astype(BF16)
        lo = (x - hi.astype(F32)).astype(BF16)
        return _dot(tril, hi) + _dot(tril, lo)

    chunks = [slice(ci * c, (ci + 1) * c) for ci in range(tm // c)]
    qs, ks, kbs = ([ref[rs, :] for rs in chunks] for ref in (q_ref, k_ref, kb_ref))
    gs = [g_ref[rs, :] for rs in chunks]
    sums = [tril_dot(jnp.concatenate([g, jnp.where(col < row, g, zero)], axis=1)) for g in gs]
    gcs = [sm[:, :w] for sm in sums]
    decays = [jnp.exp(jnp.where(col <= row, sm[:, w:], NEG_BIG)) for sm in sums]
    egcs = [jnp.exp(gc) for gc in gcs]
    grams = [_heads_gram(jnp.concatenate([kb, qc], axis=0), kc) for kb, qc, kc in zip(kbs, qs, ks)]
    t_alls = _unit_lower_inverse([jnp.where(col < row, gr[:c] * dc, zero) for gr, dc in zip(grams, decays)],
                                 row, col)
    attns = [gr[c:] * dc for gr, dc in zip(grams, decays)]
    wus = [_heads_matmul(t, kb.astype(F32) * e, vb_ref[rs, :]) for t, kb, e, rs in zip(t_alls, kbs, egcs, chunks)]
    w_alls, u_alls = [wu[0] for wu in wus], [wu[1] for wu in wus]
    q_decs = [qc.astype(F32) * e for qc, e in zip(qs, egcs)]
    k_decs = [kc.astype(F32) * jnp.exp(gc[c - 1:c, :] - gc) for kc, gc in zip(ks, gcs)]
    s_decays = [jnp.exp(gc[c - 1:c, :]) for gc in gcs]

    attn_wu = [_heads_matmul(at, wa, ua) for at, wa, ua in zip(attns, w_alls, u_alls)]
    q_effs = [qd - awu[0] for qd, awu in zip(q_decs, attn_wu)]
    o_locals = [awu[1] for awu in attn_wu]
    k_wu = [_heads_outer(kd, wa, ua) for kd, wa, ua in zip(k_decs, w_alls, u_alls)]
    kws, kus = [x[0] for x in k_wu], [x[1] for x in k_wu]

    s_all = s_ref[...]
    outs = []
    for ci in range(tm // c):
        prod = _heads_matmul(jnp.concatenate([q_effs[ci], kws[ci]], axis=0), s_all)
        outs.append(prod[:c] + o_locals[ci])
        s_all = s_all * s_decays[ci] - prod[c:] + kus[ci]
    s_ref[...] = s_all

    o = jnp.concatenate(outs, axis=0)
    var = _dot((o * o).astype(BF16), ones_ref[...]) * (1.0 / GDN_HEAD_DIM)
    o_ref[...] = (o * lax.rsqrt(var + EPS) * ng_ref[...] * gate_ref[...]).astype(BF16)


def _gdn_call(gq, gk, gkb, gvb, g_e, gate, ng_e, ones_bd, *, batch, seq, tm=512):
    nt = seq // tm
    row = pl.BlockSpec((tm, GDN_W), lambda b, j: (b * nt + j, 0))
    consts = [ng_e, ones_bd]
    return pl.pallas_call(
        functools.partial(_gdn_kernel, tm=tm),
        grid=(batch, nt),
        in_specs=[row] * 6 + [_const_spec(x.shape) for x in consts],
        out_specs=row,
        out_shape=jax.ShapeDtypeStruct((batch * seq, GDN_W), BF16),
        scratch_shapes=[pltpu.VMEM((GDN_HEAD_DIM, GDN_W), F32)],
        compiler_params=pltpu.CompilerParams(dimension_semantics=("arbitrary", "arbitrary"),
                                             vmem_limit_bytes=VMEM_LIMIT),
        name="gdn",
    )(gq, gk, gkb, gvb, g_e, gate, *consts)


def _rotate_half_cols(w):
    half = MLA_ROPE_DIM // 2
    return jnp.concatenate([-w[:, half:], w[:, :half]], axis=1)


def _prep_proj_weights(w_in, w_uq, w_ukv):
    d = w_in.shape[0]
    sizes = (MLA_Q_RANK, MLA_KV_RANK, MLA_ROPE_DIM, 3 * GDN_W, N_GDN_HEADS, N_GDN_HEADS, GDN_W)
    cuts = np.cumsum(sizes)[:-1]
    w_cq, w_ckv, w_kpe, w_qkv, w_a, w_b, w_gate = jnp.split(w_in, cuts, axis=1)

    def rope_block(wpe):
        z_lo = jnp.zeros((d, MLA_NOPE_DIM), F32)
        z_hi = jnp.zeros((d, HEAD_PAD - MLA_NOPE_DIM - MLA_ROPE_DIM), F32)
        return jnp.concatenate([z_lo, wpe, z_hi], axis=1)

    w_all = jnp.concatenate([
        w_cq, w_ckv, rope_block(w_kpe), rope_block(_rotate_half_cols(w_kpe)), w_qkv, w_gate,
        jnp.repeat(w_a, GDN_HEAD_DIM, axis=1), jnp.repeat(w_b, GDN_HEAD_DIM, axis=1)], axis=1).astype(BF16)

    qh = w_uq.reshape(MLA_Q_RANK, N_MLA_HEADS, MLA_NOPE_DIM + MLA_ROPE_DIM)
    q_nope, q_pe = qh[..., :MLA_NOPE_DIM], qh[..., MLA_NOPE_DIM:]
    q_pe_rot = jnp.concatenate([-q_pe[..., MLA_ROPE_DIM // 2:], q_pe[..., :MLA_ROPE_DIM // 2]], axis=-1)
    pad = jnp.zeros((MLA_Q_RANK, N_MLA_HEADS, HEAD_PAD - MLA_NOPE_DIM - MLA_ROPE_DIM), F32)
    wq1 = jnp.concatenate([q_nope, q_pe, pad], axis=-1).reshape(MLA_Q_RANK, -1).astype(BF16)
    wq2 = jnp.concatenate([jnp.zeros_like(q_nope), q_pe_rot, pad], axis=-1).reshape(MLA_Q_RANK, -1).astype(BF16)

    kvh = w_ukv.reshape(MLA_KV_RANK, N_MLA_HEADS, MLA_NOPE_DIM + MLA_V_DIM)
    k_nope, v = kvh[..., :MLA_NOPE_DIM], kvh[..., MLA_NOPE_DIM:]
    kpad = jnp.zeros((MLA_KV_RANK, N_MLA_HEADS, HEAD_PAD - MLA_NOPE_DIM), F32)
    wk = jnp.concatenate([k_nope, kpad], axis=-1).reshape(MLA_KV_RANK, -1).astype(BF16)
    vp = v.reshape(MLA_KV_RANK, N_MLA_HEADS // 2, 2, MLA_V_DIM)
    vz = jnp.zeros_like(vp[:, :, 0])
    wv = jnp.concatenate([vp[:, :, 0], vz, vz, vp[:, :, 1]], axis=-1).reshape(MLA_KV_RANK, -1).T.astype(BF16)
    return w_all, wq1, wq2, wk, wv


def _value_ones_rows():
    pair = jnp.concatenate([jnp.zeros((MLA_V_DIM,), F32), jnp.ones((2 * MLA_V_DIM,), F32),
                            jnp.zeros((MLA_V_DIM,), F32)])
    return jnp.tile(pair, N_MLA_HEADS // 2)[:, None]


def _rope_constants():
    half = MLA_ROPE_DIM // 2
    freqs = ROPE_THETA ** (-jnp.arange(half, dtype=F32) / half)
    freq_packed = jnp.tile(freqs, LANES // half)[None, :]
    j = jnp.arange(LANES) % half
    lane = jnp.arange(LANES)
    spread = ((lane[None, :] == MLA_NOPE_DIM + j[:, None])
              | (lane[None, :] == MLA_NOPE_DIM + half + j[:, None])).astype(BF16)
    nonrope = ((lane < MLA_NOPE_DIM) | (lane >= MLA_NOPE_DIM + MLA_ROPE_DIM)).astype(F32)[None, :]
    return freq_packed, spread, nonrope


def kernel(x, positions, ffn1_pre_g, ffn1_w_gate, ffn1_w_up, ffn1_w_down, ffn1_post_g, mix_pre_g, w_in, mla_q_norm_g, mla_w_uq, mla_kv_norm_g, mla_w_ukv, mla_out_g, gdn_conv_w, gdn_a_log, gdn_dt_bias, gdn_norm_g, w_out, mix_post_g, ffn2_pre_g, ffn2_w_gate, ffn2_w_up, ffn2_w_down, ffn2_post_g):
    batch, seq, d = x.shape
    n = batch * seq
    xt = x.reshape(n, d)
    pos = jnp.repeat(positions.reshape(n), MLA_ROPE_DIM // 2).reshape(n // POS_PACK, LANES)
    freq, spread, nonrope = _rope_constants()
    head_ids = jnp.arange(GDN_W) // GDN_HEAD_DIM
    ones_bd = (head_ids[:, None] == head_ids[None, :]).astype(BF16)
    bf = lambda w: w.astype(BF16)
    r = lambda g: g[None, :]

    for l in range(ffn1_pre_g.shape[0]):
        xt = _ffn_call(xt, r(ffn1_pre_g[l]), bf(ffn1_w_gate[l]), bf(ffn1_w_up[l]), bf(ffn1_w_down[l]),
                       r(ffn1_post_g[l]))
        w_all, wq1, wq2, wk, wv = _prep_proj_weights(w_in[l], mla_w_uq[l], mla_w_ukv[l])
        proj_consts = [r(mix_pre_g[l]), w_all, r(mla_q_norm_g[l]), wq1, wq2, r(mla_kv_norm_g[l]), wk, wv,
                       freq, spread, nonrope, _value_ones_rows(), gdn_conv_w[l],
                       r(jnp.repeat(gdn_a_log[l].astype(F32), GDN_HEAD_DIM)),
                       r(jnp.repeat(gdn_dt_bias[l].astype(F32), GDN_HEAD_DIM)), ones_bd]
        q, k, v, gq, gk, gkb, gvb, g_e, gate = _proj_call(xt, pos, proj_consts, seq=seq)
        mla_o = _attn_call(q, k, v, batch=batch, seq=seq)
        gdn_o = _gdn_call(gq, gk, gkb, gvb, g_e, gate, r(jnp.tile(gdn_norm_g[l], N_GDN_HEADS)), ones_bd,
                          batch=batch, seq=seq)
        xt = _ffn_call(xt, r(ffn2_pre_g[l]), bf(ffn2_w_gate[l]), bf(ffn2_w_up[l]), bf(ffn2_w_down[l]),
                       r(ffn2_post_g[l]),
                       mix=(mla_o, gdn_o, r(mla_out_g[l]), bf(w_out[l]), r(mix_post_g[l])))
    return xt.reshape(batch, seq, d)
```

```python
import functools

import jax
import jax.numpy as jnp
import numpy as np
from jax import lax
from jax.experimental import pallas as pl
from jax.experimental.pallas import tpu as pltpu

F32 = jnp.float32
BF16 = jnp.bfloat16

D_MODEL = 1024
N_MLA_HEADS = 8
MLA_Q_RANK = 256
MLA_KV_RANK = 128
MLA_NOPE_DIM = 64
MLA_ROPE_DIM = 32
MLA_V_DIM = 64
ROPE_THETA = 10000.0
N_GDN_HEADS = 8
GDN_HEAD_DIM = 64
GDN_CONV = 4
GDN_CHUNK = 64
MLA_W = N_MLA_HEADS * MLA_V_DIM
GDN_W = N_GDN_HEADS * GDN_HEAD_DIM
D_FF = 2816
EPS = 1e-6

LANES = 128
POS_PACK = LANES // (MLA_ROPE_DIM // 2)
ATTN_BK = 256
ATTN_QSUB = 512
HEAD_PAD = 128
NEG_BIG = -1e30
LOG2_E = 1.4426950408889634
CONV_HALO = 8
VMEM_LIMIT = 56 * 1024 * 1024

_PROJ_BLOCKS = (("cq", MLA_Q_RANK), ("ckv", MLA_KV_RANK), ("kpe", LANES), ("kper", LANES),
                ("qkv", 3 * GDN_W), ("gate", GDN_W), ("a", GDN_W), ("b", GDN_W))
_PROJ_OFF = {}
_off = 0
for _name, _width in _PROJ_BLOCKS:
    _PROJ_OFF[_name] = (_off, _off + _width)
    _off += _width
PROJ_COLS = _off


def _rms(x, g):
    return x * lax.rsqrt(jnp.mean(x * x, axis=-1, keepdims=True) + EPS) * g


def _dot(a, b):
    return jnp.dot(a, b, preferred_element_type=F32)


def _const_spec(shape):
    nd = len(shape)
    return pl.BlockSpec(shape, lambda *_: (0,) * nd, pipeline_mode=pl.Buffered(1))


def _swiglu_residual(x, pre_g, wg_ref, wu_ref, wd_ref, post_g, h_ref, fc):
    xn = _rms(x, pre_g).astype(BF16)
    for c in range(wg_ref.shape[1] // fc):
        sl = slice(c * fc, (c + 1) * fc)
        g = _dot(xn, wg_ref[:, sl])
        u = _dot(xn, wu_ref[:, sl])
        h_ref[:, sl] = (g * jax.nn.sigmoid(g) * u).astype(BF16)
    y = _dot(h_ref[...], wd_ref[...])
    return x + 0.5 * _rms(y, post_g)


def _ffn_kernel(x_ref, pre_g_ref, wg_ref, wu_ref, wd_ref, post_g_ref, o_ref, h_ref, *, fc):
    o_ref[...] = _swiglu_residual(x_ref[...], pre_g_ref[...], wg_ref, wu_ref, wd_ref,
                                  post_g_ref[...], h_ref, fc)


def _ffn_mix_kernel(x_ref, mla_ref, gdn_ref, mla_g_ref, wo_ref, mix_g_ref,
                    pre_g_ref, wg_ref, wu_ref, wd_ref, post_g_ref, o_ref, h_ref, *, fc):
    mla = _rms(mla_ref[...], mla_g_ref[...]).astype(BF16)
    mixed = _dot(mla, wo_ref[:MLA_W, :]) + _dot(gdn_ref[...], wo_ref[MLA_W:, :])
    x2 = x_ref[...] + _rms(mixed, mix_g_ref[...])
    o_ref[...] = _swiglu_residual(x2, pre_g_ref[...], wg_ref, wu_ref, wd_ref,
                                  post_g_ref[...], h_ref, fc)


def _ffn_call(x, pre_g, wg, wu, wd, post_g, mix=None, *, tm=1024, fc=256):
    n, d = x.shape
    f = wg.shape[1]
    row = lambda w: pl.BlockSpec((tm, w), lambda i: (i, 0))
    ffn_specs = [_const_spec((1, d)), _const_spec((d, f)), _const_spec((d, f)),
                 _const_spec((f, d)), _const_spec((1, d))]
    ffn_args = [pre_g, wg, wu, wd, post_g]
    if mix is None:
        body, in_specs, args = _ffn_kernel, [row(d)] + ffn_specs, [x] + ffn_args
    else:
        mla_o, gdn_o, mla_g, wo, mix_g = mix
        body = _ffn_mix_kernel
        in_specs = [row(d), row(MLA_W), row(GDN_W), _const_spec((1, MLA_W)),
                    _const_spec(wo.shape), _const_spec((1, d))] + ffn_specs
        args = [x, mla_o, gdn_o, mla_g, wo, mix_g] + ffn_args
    return pl.pallas_call(
        functools.partial(body, fc=fc),
        grid=(n // tm,),
        in_specs=in_specs,
        out_specs=row(d),
        out_shape=jax.ShapeDtypeStruct((n, d), F32),
        scratch_shapes=[pltpu.VMEM((tm, f), BF16)],
        compiler_params=pltpu.CompilerParams(dimension_semantics=("arbitrary",),
                                             vmem_limit_bytes=VMEM_LIMIT),
        name="ffn_mix" if mix is not None else "ffn",
    )(*args)


def _proj_kernel(x_ref, pos_ref, pre_g_ref, w_ref, qg_ref, wq1_ref, wq2_ref, kvg_ref, wk_ref, wv_ref,
                 freq_ref, spread_ref, nonrope_ref, vones_ref, convw_ref, alog_ref, dtb_ref, ones_ref,
                 q_ref, k_ref, v_ref, gq_ref, gk_ref, gkb_ref, gvb_ref, g_ref, gate_ref,
                 xbuf_ref, *, tiles_per_seq):
    hn = _rms(x_ref[...], pre_g_ref[...]).astype(BF16)

    def blk(name):
        lo, hi = _PROJ_OFF[name]
        return _dot(hn, w_ref[:, lo:hi])

    w = GDN_W
    tm = x_ref.shape[0]

    @pl.when(pl.program_id(0) % tiles_per_seq == 0)
    def _():
        xbuf_ref[0:CONV_HALO, :] = jnp.zeros((CONV_HALO, 3 * w), F32)

    xbuf_ref[CONV_HALO:CONV_HALO + tm, :] = blk("qkv")
    xe = xbuf_ref[...]
    y = xe[CONV_HALO:, :] * convw_ref[GDN_CONV - 1:GDN_CONV, :]
    for back in range(1, GDN_CONV):
        shifted = pltpu.roll(xe, back, axis=0)[CONV_HALO:, :]
        y = y + shifted * convw_ref[GDN_CONV - 1 - back:GDN_CONV - back, :]
    xbuf_ref[0:CONV_HALO, :] = xe[tm:, :]
    y = y * jax.nn.sigmoid(y)

    def head_sumsq(x):
        return _dot((x * x).astype(BF16), ones_ref[...])

    gq, gk, gv = y[:, 0:w], y[:, w:2 * w], y[:, 2 * w:3 * w]
    gq_ref[...] = (gq * lax.rsqrt(head_sumsq(gq) + EPS) * (GDN_HEAD_DIM ** -0.5)).astype(BF16)
    gk = gk * lax.rsqrt(head_sumsq(gk) + EPS)
    gk_ref[...] = gk.astype(BF16)
    beta = jax.nn.sigmoid(blk("b"))
    gkb_ref[...] = (gk * beta).astype(BF16)
    gvb_ref[...] = (gv * beta).astype(BF16)
    sp_in = blk("a") + dtb_ref[...]
    softplus = jnp.maximum(sp_in, 0.0) + jnp.log1p(jnp.exp(-jnp.abs(sp_in)))
    g_ref[...] = -jnp.exp(alog_ref[...]) * softplus
    gate = blk("gate")
    gate_ref[...] = gate * jax.nn.sigmoid(gate)

    ang = pos_ref[...].astype(F32) * freq_ref[...]
    tok = lax.broadcasted_iota(jnp.int32, (tm, LANES), 0) % POS_PACK
    own = lax.broadcasted_iota(jnp.int32, (tm, LANES), 1) // (MLA_ROPE_DIM // 2) == tok

    def to_rope_lanes(packed):
        per_tok = jnp.broadcast_to(packed[:, None, :], (tm // POS_PACK, POS_PACK, LANES)).reshape(tm, LANES)
        per_tok = jnp.where(own, per_tok, 0.0)
        hi = per_tok.astype(BF16)
        lo = (per_tok - hi.astype(F32)).astype(BF16)
        return _dot(hi, spread_ref[...]) + _dot(lo, spread_ref[...])

    sin = to_rope_lanes(jnp.sin(ang))
    cos = to_rope_lanes(jnp.cos(ang)) + nonrope_ref[...]
    scale = (MLA_NOPE_DIM + MLA_ROPE_DIM) ** -0.5 * LOG2_E

    cq = _rms(blk("cq"), qg_ref[...]).astype(BF16)
    reps = (1, N_MLA_HEADS)
    q = _dot(cq, wq1_ref[...]) * jnp.tile(cos * scale, reps) + _dot(cq, wq2_ref[...]) * jnp.tile(sin * scale, reps)
    q_ref[...] = q.astype(BF16)

    ckv = _rms(blk("ckv"), kvg_ref[...]).astype(BF16)
    kpe = blk("kpe") * cos + blk("kper") * sin
    k_ref[...] = (_dot(ckv, wk_ref[...]) + jnp.tile(kpe, reps)).astype(BF16)
    vt = (lax.dot_general(wv_ref[...], ckv, (((1,), (1,)), ((), ())), preferred_element_type=F32)
          + vones_ref[...]).astype(BF16)
    for kb in range(tm // ATTN_BK):
        v_ref[kb] = vt[:, kb * ATTN_BK:(kb + 1) * ATTN_BK]


def _proj_call(x1, pos, consts, *, seq, tm=512):
    n, d = x1.shape
    row = lambda w: pl.BlockSpec((tm, w), lambda i: (i, 0))
    hw = N_MLA_HEADS * HEAD_PAD
    outs = [(hw, BF16), (hw, BF16),
            (GDN_W, BF16), (GDN_W, BF16), (GDN_W, BF16), (GDN_W, BF16), (GDN_W, F32), (GDN_W, F32)]
    vt_spec = pl.BlockSpec((tm // ATTN_BK, hw, ATTN_BK), lambda i: (i, 0, 0))
    vt_shape = jax.ShapeDtypeStruct((n // ATTN_BK, hw, ATTN_BK), BF16)
    return pl.pallas_call(
        functools.partial(_proj_kernel, tiles_per_seq=seq // tm),
        grid=(n // tm,),
        in_specs=([row(d), pl.BlockSpec((tm // POS_PACK, LANES), lambda i: (i, 0))]
                  + [_const_spec(c.shape) for c in consts]),
        out_specs=[row(w) for w, _ in outs[:2]] + [vt_spec] + [row(w) for w, _ in outs[2:]],
        out_shape=([jax.ShapeDtypeStruct((n, w), dt) for w, dt in outs[:2]] + [vt_shape]
                   + [jax.ShapeDtypeStruct((n, w), dt) for w, dt in outs[2:]]),
        scratch_shapes=[pltpu.VMEM((tm + CONV_HALO, 3 * GDN_W), F32)],
        compiler_params=pltpu.CompilerParams(dimension_semantics=("arbitrary",),
                                             vmem_limit_bytes=VMEM_LIMIT),
        name="proj",
    )(x1, pos, *consts)


def _attn_kernel(q_ref, k_ref, vt_ref, o_ref, m_ref, acc_ref, s_ref, *, seq, bq):
    bk = ATTN_BK
    nk = bq // bk
    heads = [slice(hh * HEAD_PAD, (hh + 1) * HEAD_PAD) for hh in range(2)]

    def kv_rows(j):
        return pl.ds(pl.multiple_of(j * bk, bk), bk)

    def scores(qs, j, buf, q_cols=slice(None)):
        for hh, sl in enumerate(heads):
            s_ref[buf, hh, :, q_cols] = lax.dot_general(k_ref[kv_rows(j), sl], qs[hh][q_cols],
                                                        (((1,), (1,)), ((), ())), preferred_element_type=F32)

    def accumulate(j, buf, q_cols=slice(None), mask=None):
        start = q_cols.start or 0
        stop = q_cols.stop or bq
        for hh, sl in enumerate(heads):
            for c0 in range(start, stop, ATTN_QSUB):
                cs = slice(c0, min(c0 + ATTN_QSUB, stop))
                s = s_ref[buf, hh, :, cs]
                if mask is not None:
                    s = jnp.where(mask[:, c0 - start:cs.stop - start], s, NEG_BIG)
                m_old = m_ref[hh, :, cs]
                m_new = jnp.maximum(m_old, jnp.max(s, axis=0, keepdims=True))
                p = jnp.exp2(s - m_new)
                acc_ref[hh, :, cs] = (jnp.exp2(m_old - m_new) * acc_ref[hh, :, cs]
                                      + _dot(vt_ref[j, sl, :], p.astype(BF16)))
                m_ref[hh, :, cs] = m_new

    def load_q(qi):
        return [q_ref[qi * bq:(qi + 1) * bq, sl] for sl in heads]

    nq = seq // bq
    qs = load_q(0)
    scores(qs, 0, 0)
    for qi in range(nq):
        m_ref[...] = jnp.full(m_ref.shape, NEG_BIG, F32)
        acc_ref[...] = jnp.zeros(acc_ref.shape, F32)

        def body(t, c, qs=qs):
            for d in range(nk):
                scores(qs, nk * t + d + 1, (d + 1) % 2)
                accumulate(nk * t + d, d % 2)
            return c

        lax.fori_loop(0, qi, body, 0)
        qs_next = load_q(qi + 1) if qi + 1 < nq else None
        for d in range(nk):
            if d + 1 < nk:
                scores(qs, nk * qi + d + 1, (d + 1) % 2, q_cols=slice((d + 1) * bk, bq))
            elif qs_next is not None:
                scores(qs_next, 0, (d + 1) % 2)
            accumulate(nk * qi + d, d % 2, q_cols=slice(d * bk, bq),
                       mask=(lax.broadcasted_iota(jnp.int32, (bk, bq - d * bk), 0)
                             <= lax.broadcasted_iota(jnp.int32, (bk, bq - d * bk), 1)))
        a0, a1 = acc_ref[0], acc_ref[1]
        num = jnp.concatenate([a0[:MLA_V_DIM], a1[MLA_V_DIM:]], axis=0)
        den = jnp.concatenate([a0[MLA_V_DIM:], a1[:MLA_V_DIM]], axis=0)
        o_ref[qi * bq:(qi + 1) * bq, :] = (num / den).T
        qs = qs_next


def _attn_call(q, k, vt, *, batch, seq, bq=1024):
    pairs = N_MLA_HEADS // 2
    in_spec = pl.BlockSpec((seq, 2 * HEAD_PAD), lambda b, p: (b, p))
    vt_spec = pl.BlockSpec((seq // ATTN_BK, 2 * HEAD_PAD, ATTN_BK), lambda b, p: (b, p, 0))
    return pl.pallas_call(
        functools.partial(_attn_kernel, seq=seq, bq=bq),
        grid=(batch, pairs),
        in_specs=[in_spec, in_spec, vt_spec],
        out_specs=pl.BlockSpec((seq, LANES), lambda b, p: (b, p)),
        out_shape=jax.ShapeDtypeStruct((batch * seq, MLA_W), F32),
        scratch_shapes=[pltpu.VMEM((2, 1, bq), F32), pltpu.VMEM((2, HEAD_PAD, bq), F32),
                        pltpu.VMEM((2, 2, ATTN_BK, bq), F32)],
        compiler_params=pltpu.CompilerParams(dimension_semantics=("arbitrary", "arbitrary"),
                                             vmem_limit_bytes=VMEM_LIMIT),
        name="attn",
    )(q, k, vt)


GROUP_HEADS = LANES // GDN_HEAD_DIM
GROUP_LANES = GROUP_HEADS * GDN_HEAD_DIM
GROUP_SLICES = tuple(slice(i * GROUP_LANES, (i + 1) * GROUP_LANES) for i in range(GDN_W // GROUP_LANES))


def _head_in_group(shape, dtype=jnp.int32):
    lane = lax.broadcasted_iota(jnp.int32, shape, 1) % GROUP_LANES // GDN_HEAD_DIM
    return lane.astype(F32).astype(dtype)


def _group_blockdiag(x_grp):
    x_grp = x_grp.astype(BF16)
    head = _head_in_group(x_grp.shape, BF16)
    zero = jnp.zeros_like(x_grp)
    return jnp.concatenate([jnp.where(head == h, x_grp, zero) for h in range(GROUP_HEADS)], axis=0)


def _heads_matmul(a_all, *b_alls):
    outs = [[] for _ in b_alls]
    for sl in GROUP_SLICES:
        rhs = jnp.concatenate([_group_blockdiag(b[:, sl]) for b in b_alls], axis=1)
        prod = _dot(a_all[:, sl].astype(BF16), rhs)
        for k, out in enumerate(outs):
            out.append(prod[:, k * GROUP_LANES:(k + 1) * GROUP_LANES])
    outs = [jnp.concatenate(out, axis=1) for out in outs]
    return outs[0] if len(outs) == 1 else outs


def _heads_gram(a_all, b_all):
    return jnp.concatenate([lax.dot_general(a_all[:, sl].astype(BF16), _group_blockdiag(b_all[:, sl]),
                                            (((1,), (1,)), ((), ())), preferred_element_type=F32)
                            for sl in GROUP_SLICES], axis=1)


def _heads_outer(a_all, *b_alls):
    outs = [[] for _ in b_alls]
    head = _head_in_group((GDN_HEAD_DIM, GROUP_LANES))
    for sl in GROUP_SLICES:
        rhs = jnp.concatenate([b[:, sl].astype(BF16) for b in b_alls], axis=1)
        full = lax.dot_general(a_all[:, sl].astype(BF16), rhs,
                               (((0,), (0,)), ((), ())), preferred_element_type=F32)
        for k, out in enumerate(outs):
            blk = full[:, k * GROUP_LANES:(k + 1) * GROUP_LANES]
            diag = blk[:GDN_HEAD_DIM]
            for h in range(1, GROUP_HEADS):
                diag = jnp.where(head == h, blk[h * GDN_HEAD_DIM:(h + 1) * GDN_HEAD_DIM], diag)
            out.append(diag)
    outs = [jnp.concatenate(out, axis=1) for out in outs]
    return outs[0] if len(outs) == 1 else outs


def _unit_lower_inverse(l_all, row, col):
    eye = jnp.where(row == col, 1.0, 0.0).astype(F32)
    zero = jnp.zeros_like(row, dtype=F32)
    same16 = (row // 16) == (col // 16)
    same32 = (row // 32) == (col // 32)
    mm = lambda xs, ys: [_heads_matmul(x, y) for x, y in zip(xs, ys)]
    m1 = [jnp.where(same16, -l, zero) for l in l_all]
    inv = [eye + m for m in m1]
    power = mm(m1, m1)
    for _ in range(2):
        both = [_heads_matmul(p, i, p) for p, i in zip(power, inv)]
        inv = [i + b[0] for i, b in zip(inv, both)]
        power = [b[1] for b in both]
    inv = [i + pi for i, pi in zip(inv, mm(power, inv))]
    for pick in (same32 & ~same16, ~same32):
        off = [jnp.where(pick, l, zero) for l in l_all]
        inv = [i - x for i, x in zip(inv, mm(inv, mm(off, inv)))]
    return inv


def _gdn_kernel(q_ref, k_ref, kb_ref, vb_ref, g_ref, gate_ref, ng_ref, ones_ref, o_ref, s_ref, *, tm):
    c = GDN_CHUNK
    w = GDN_W

    @pl.when(pl.program_id(1) == 0)
    def _():
        s_ref[...] = jnp.zeros(s_ref.shape, F32)

    row = lax.broadcasted_iota(jnp.int32, (c, w), 0)
    col = lax.broadcasted_iota(jnp.int32, (c, w), 1) % GDN_HEAD_DIM
    tril = jnp.where(lax.broadcasted_iota(jnp.int32, (c, c), 1) <= lax.broadcasted_iota(jnp.int32, (c, c), 0),
                     1.0, 0.0).astype(BF16)
    zero = jnp.zeros((c, w), F32)

    def tril_dot(x):
        hi = x.astype(BF16)
        lo = (x - hi.astype(F32)).astype(BF16)
        return _dot(tril, hi) + _dot(tril, lo)

    chunks = [slice(ci * c, (ci + 1) * c) for ci in range(tm // c)]
    qs, ks, kbs = ([ref[rs, :] for rs in chunks] for ref in (q_ref, k_ref, kb_ref))
    gs = [g_ref[rs, :] for rs in chunks]
    sums = [tril_dot(jnp.concatenate([g, jnp.where(col < row, g, zero)], axis=1)) for g in gs]
    gcs = [sm[:, :w] for sm in sums]
    decays = [jnp.exp(jnp.where(col <= row, sm[:, w:], NEG_BIG)) for sm in sums]
    egcs = [jnp.exp(gc) for gc in gcs]
    grams = [_heads_gram(jnp.concatenate([kb, qc], axis=0), kc) for kb, qc, kc in zip(kbs, qs, ks)]
    t_alls = _unit_lower_inverse([jnp.where(col < row, gr[:c] * dc, zero) for gr, dc in zip(grams, decays)],
                                 row, col)
    attns = [gr[c:] * dc for gr, dc in zip(grams, decays)]
    wus = [_heads_matmul(t, kb.astype(F32) * e, vb_ref[rs, :]) for t, kb, e, rs in zip(t_alls, kbs, egcs, chunks)]
    w_alls, u_alls = [wu[0] for wu in wus], [wu[1] for wu in wus]
    q_decs = [qc.astype(F32) * e for qc, e in zip(qs, egcs)]
    k_decs = [kc.astype(F32) * jnp.exp(gc[c - 1:c, :] - gc) for kc, gc in zip(ks, gcs)]
    s_decays = [jnp.exp(gc[c - 1:c, :]) for gc in gcs]

    attn_wu = [_heads_matmul(at, wa, ua) for at, wa, ua in zip(attns, w_alls, u_alls)]
    q_effs = [qd - awu[0] for qd, awu in zip(q_decs, attn_wu)]
    o_locals = [awu[1] for awu in attn_wu]
    k_wu = [_heads_outer(kd, wa, ua) for kd, wa, ua in zip(k_decs, w_alls, u_alls)]
    kws, kus = [x[0] for x in k_wu], [x[1] for x in k_wu]

    s_all = s_ref[...]
    outs = []
    for ci in range(tm // c):
        prod = _heads_matmul(jnp.concatenate([q_effs[ci], kws[ci]], axis=0), s_all)
        outs.append(prod[:c] + o_locals[ci])
        s_all = s_all * s_decays[ci] - prod[c:] + kus[ci]
    s_ref[...] = s_all

    o = jnp.concatenate(outs, axis=0)
    var = _dot((o * o).astype(BF16), ones_ref[...]) * (1.0 / GDN_HEAD_DIM)
    o_ref[...] = (o * lax.rsqrt(var + EPS) * ng_ref[...] * gate_ref[...]).astype(BF16)


def _gdn_call(gq, gk, gkb, gvb, g_e, gate, ng_e, ones_bd, *, batch, seq, tm=512):
    nt = seq // tm
    row = pl.BlockSpec((tm, GDN_W), lambda b, j: (b * nt + j, 0))
    consts = [ng_e, ones_bd]
    return pl.pallas_call(
        functools.partial(_gdn_kernel, tm=tm),
        grid=(batch, nt),
        in_specs=[row] * 6 + [_const_spec(x.shape) for x in consts],
        out_specs=row,
        out_shape=jax.ShapeDtypeStruct((batch * seq, GDN_W), BF16),
        scratch_shapes=[pltpu.VMEM((GDN_HEAD_DIM, GDN_W), F32)],
        compiler_params=pltpu.CompilerParams(dimension_semantics=("arbitrary", "arbitrary"),
                                             vmem_limit_bytes=VMEM_LIMIT),
        name="gdn",
    )(gq, gk, gkb, gvb, g_e, gate, *consts)


def _rotate_half_cols(w):
    half = MLA_ROPE_DIM // 2
    return jnp.concatenate([-w[:, half:], w[:, :half]], axis=1)


def _prep_proj_weights(w_in, w_uq, w_ukv):
    d = w_in.shape[0]
    sizes = (MLA_Q_RANK, MLA_KV_RANK, MLA_ROPE_DIM, 3 * GDN_W, N_GDN_HEADS, N_GDN_HEADS, GDN_W)
    cuts = np.cumsum(sizes)[:-1]
    w_cq, w_ckv, w_kpe, w_qkv, w_a, w_b, w_gate = jnp.split(w_in, cuts, axis=1)

    def rope_block(wpe):
        z_lo = jnp.zeros((d, MLA_NOPE_DIM), F32)
        z_hi = jnp.zeros((d, HEAD_PAD - MLA_NOPE_DIM - MLA_ROPE_DIM), F32)
        return jnp.concatenate([z_lo, wpe, z_hi], axis=1)

    w_all = jnp.concatenate([
        w_cq, w_ckv, rope_block(w_kpe), rope_block(_rotate_half_cols(w_kpe)), w_qkv, w_gate,
        jnp.repeat(w_a, GDN_HEAD_DIM, axis=1), jnp.repeat(w_b, GDN_HEAD_DIM, axis=1)], axis=1).astype(BF16)

    qh = w_uq.reshape(MLA_Q_RANK, N_MLA_HEADS, MLA_NOPE_DIM + MLA_ROPE_DIM)
    q_nope, q_pe = qh[..., :MLA_NOPE_DIM], qh[..., MLA_NOPE_DIM:]
    q_pe_rot = jnp.concatenate([-q_pe[..., MLA_ROPE_DIM // 2:], q_pe[..., :MLA_ROPE_DIM // 2]], axis=-1)
    pad = jnp.zeros((MLA_Q_RANK, N_MLA_HEADS, HEAD_PAD - MLA_NOPE_DIM - MLA_ROPE_DIM), F32)
    wq1 = jnp.concatenate([q_nope, q_pe, pad], axis=-1).reshape(MLA_Q_RANK, -1).astype(BF16)
    wq2 = jnp.concatenate([jnp.zeros_like(q_nope), q_pe_rot, pad], axis=-1).reshape(MLA_Q_RANK, -1).astype(BF16)

    kvh = w_ukv.reshape(MLA_KV_RANK, N_MLA_HEADS, MLA_NOPE_DIM + MLA_V_DIM)
    k_nope, v = kvh[..., :MLA_NOPE_DIM], kvh[..., MLA_NOPE_DIM:]
    kpad = jnp.zeros((MLA_KV_RANK, N_MLA_HEADS, HEAD_PAD - MLA_NOPE_DIM), F32)
    wk = jnp.concatenate([k_nope, kpad], axis=-1).reshape(MLA_KV_RANK, -1).astype(BF16)
    vp = v.reshape(MLA_KV_RANK, N_MLA_HEADS // 2, 2, MLA_V_DIM)
    vz = jnp.zeros_like(vp[:, :, 0])
    wv = jnp.concatenate([vp[:, :, 0], vz, vz, vp[:, :, 1]], axis=-1).reshape(MLA_KV_RANK, -1).T.astype(BF16)
    return w_all, wq1, wq2, wk, wv


def _value_ones_rows():
    pair = jnp.concatenate([jnp.zeros((MLA_V_DIM,), F32), jnp.ones((2 * MLA_V_DIM,), F32),
                            jnp.zeros((MLA_V_DIM,), F32)])
    return jnp.tile(pair, N_MLA_HEADS // 2)[:, None]


def _rope_constants():
    half = MLA_ROPE_DIM // 2
    freqs = ROPE_THETA ** (-jnp.arange(half, dtype=F32) / half)
    freq_packed = jnp.tile(freqs, LANES // half)[None, :]
    j = jnp.arange(LANES) % half
    lane = jnp.arange(LANES)
    spread = ((lane[None, :] == MLA_NOPE_DIM + j[:, None])
              | (lane[None, :] == MLA_NOPE_DIM + half + j[:, None])).astype(BF16)
    nonrope = ((lane < MLA_NOPE_DIM) | (lane >= MLA_NOPE_DIM + MLA_ROPE_DIM)).astype(F32)[None, :]
    return freq_packed, spread, nonrope


def kernel(x, positions, ffn1_pre_g, ffn1_w_gate, ffn1_w_up, ffn1_w_down, ffn1_post_g, mix_pre_g, w_in, mla_q_norm_g, mla_w_uq, mla_kv_norm_g, mla_w_ukv, mla_out_g, gdn_conv_w, gdn_a_log, gdn_dt_bias, gdn_norm_g, w_out, mix_post_g, ffn2_pre_g, ffn2_w_gate, ffn2_w_up, ffn2_w_down, ffn2_post_g):
    batch, seq, d = x.shape
    n = batch * seq
    xt = x.reshape(n, d)
    pos = jnp.repeat(positions.reshape(n), MLA_ROPE_DIM // 2).reshape(n // POS_PACK, LANES)
    freq, spread, nonrope = _rope_constants()
    head_ids = jnp.arange(GDN_W) // GDN_HEAD_DIM
    ones_bd = (head_ids[:, None] == head_ids[None, :]).astype(BF16)
    bf = lambda w: w.astype(BF16)
    r = lambda g: g[None, :]

    for l in range(ffn1_pre_g.shape[0]):
        xt = _ffn_call(xt, r(ffn1_pre_g[l]), bf(ffn1_w_gate[l]), bf(ffn1_w_up[l]), bf(ffn1_w_down[l]),
                       r(ffn1_post_g[l]))
        w_all, wq1, wq2, wk, wv = _prep_proj_weights(w_in[l], mla_w_uq[l], mla_w_ukv[l])
        proj_consts = [r(mix_pre_g[l]), w_all, r(mla_q_norm_g[l]), wq1, wq2, r(mla_kv_norm_g[l]), wk, wv,
                       freq, spread, nonrope, _value_ones_rows(), gdn_conv_w[l],
                       r(jnp.repeat(gdn_a_log[l].astype(F32), GDN_HEAD_DIM)),
                       r(jnp.repeat(gdn_dt_bias[l].astype(F32), GDN_HEAD_DIM)), ones_bd]
        q, k, v, gq, gk, gkb, gvb, g_e, gate = _proj_call(xt, pos, proj_consts, seq=seq)
        mla_o = _attn_call(q, k, v, batch=batch, seq=seq)
        gdn_o = _gdn_call(gq, gk, gkb, gvb, g_e, gate, r(jnp.tile(gdn_norm_g[l], N_GDN_HEADS)), ones_bd,
                          batch=batch, seq=seq)
        xt = _ffn_call(xt, r(ffn2_pre_g[l]), bf(ffn2_w_gate[l]), bf(ffn2_w_up[l]), bf(ffn2_w_down[l]),
                       r(ffn2_post_g[l]),
                       mix=(mla_o, gdn_o, r(mla_out_g[l]), bf(w_out[l]), r(mix_post_g[l])))
    return xt.reshape(batch, seq, d)
```

```python
import functools

import jax
import jax.numpy as jnp
import numpy as np
from jax import lax
from jax.experimental import pallas as pl
from jax.experimental.pallas import tpu as pltpu

F32 = jnp.float32
BF16 = jnp.bfloat16

D_MODEL = 1024
N_MLA_HEADS = 8
MLA_Q_RANK = 256
MLA_KV_RANK = 128
MLA_NOPE_DIM = 64
MLA_ROPE_DIM = 32
MLA_V_DIM = 64
ROPE_THETA = 10000.0
N_GDN_HEADS = 8
GDN_HEAD_DIM = 64
GDN_CONV = 4
GDN_CHUNK = 64
MLA_W = N_MLA_HEADS * MLA_V_DIM
GDN_W = N_GDN_HEADS * GDN_HEAD_DIM
D_FF = 2816
EPS = 1e-6

LANES = 128
POS_PACK = LANES // (MLA_ROPE_DIM // 2)
FFN_SUBTILES = 2
GDN_SEQS = 4
ATTN_BK = 256
ATTN_QSUB = 512
HEAD_PAD = 128
NEG_BIG = -1e30
LOG2_E = 1.4426950408889634
CONV_HALO = 8
VMEM_LIMIT = 56 * 1024 * 1024

_PROJ_BLOCKS = (("cq", MLA_Q_RANK), ("ckv", MLA_KV_RANK), ("kpe", LANES), ("kper", LANES),
                ("qkv", 3 * GDN_W), ("gate", GDN_W), ("a", GDN_W), ("b", GDN_W))
_PROJ_OFF = {}
_off = 0
for _name, _width in _PROJ_BLOCKS:
    _PROJ_OFF[_name] = (_off, _off + _width)
    _off += _width
PROJ_COLS = _off


def _rms(x, g):
    return x * lax.rsqrt(jnp.mean(x * x, axis=-1, keepdims=True) + EPS) * g


def _silu_of_half(z):
    return z + z * jnp.tanh(z)


def _dot(a, b):
    return jnp.dot(a, b, preferred_element_type=F32)


def _const_spec(shape):
    nd = len(shape)
    return pl.BlockSpec(shape, lambda *_: (0,) * nd, pipeline_mode=pl.Buffered(1))


def _swiglu_residual(x, pre_g, wg_ref, wu_ref, wd_ref, post_g, h_ref, fc):
    xn = _rms(x, pre_g).astype(BF16)
    for c in range(wg_ref.shape[1] // fc):
        sl = slice(c * fc, (c + 1) * fc)
        g = _dot(xn, wg_ref[:, sl])
        u = _dot(xn, wu_ref[:, sl])
        h_ref[:, sl] = (_silu_of_half(g) * u).astype(BF16)
    y = _dot(h_ref[...], wd_ref[...])
    return x + 0.5 * _rms(y, post_g)


def _ffn_kernel(x_ref, pre_g_ref, wg_ref, wu_ref, wd_ref, post_g_ref, o_ref, h_ref, *, fc):
    sub = x_ref.shape[0] // FFN_SUBTILES
    for r in range(FFN_SUBTILES):
        rows = slice(r * sub, (r + 1) * sub)
        o_ref[rows, :] = _swiglu_residual(x_ref[rows, :], pre_g_ref[...], wg_ref, wu_ref, wd_ref,
                                          post_g_ref[...], h_ref.at[rows, :], fc)


def _ffn_mix_kernel(x_ref, mla_ref, gdn_ref, mla_g_ref, wo_ref, mix_g_ref,
                    pre_g_ref, wg_ref, wu_ref, wd_ref, post_g_ref, o_ref, h_ref, *, fc):
    sub = x_ref.shape[0] // FFN_SUBTILES
    for r in range(FFN_SUBTILES):
        rows = slice(r * sub, (r + 1) * sub)
        mla = _rms(mla_ref[rows, :], mla_g_ref[...]).astype(BF16)
        mixed = _dot(mla, wo_ref[:MLA_W, :]) + _dot(gdn_ref[rows, :], wo_ref[MLA_W:, :])
        x2 = x_ref[rows, :] + _rms(mixed, mix_g_ref[...])
        o_ref[rows, :] = _swiglu_residual(x2, pre_g_ref[...], wg_ref, wu_ref, wd_ref,
                                          post_g_ref[...], h_ref.at[rows, :], fc)


def _ffn_call(x, pre_g, wg, wu, wd, post_g, mix=None, *, tm=1024, fc=256):
    n, d = x.shape
    f = wg.shape[1]
    row = lambda w: pl.BlockSpec((tm, w), lambda i: (i, 0))
    ffn_specs = [_const_spec((1, d)), _const_spec((d, f)), _const_spec((d, f)),
                 _const_spec((f, d)), _const_spec((1, d))]
    ffn_args = [pre_g, wg, wu, wd, post_g]
    if mix is None:
        body, in_specs, args = _ffn_kernel, [row(d)] + ffn_specs, [x] + ffn_args
    else:
        mla_o, gdn_o, mla_g, wo, mix_g = mix
        body = _ffn_mix_kernel
        in_specs = [row(d), row(MLA_W), row(GDN_W), _const_spec((1, MLA_W)),
                    _const_spec(wo.shape), _const_spec((1, d))] + ffn_specs
        args = [x, mla_o, gdn_o, mla_g, wo, mix_g] + ffn_args
    return pl.pallas_call(
        functools.partial(body, fc=fc),
        grid=(n // tm,),
        in_specs=in_specs,
        out_specs=row(d),
        out_shape=jax.ShapeDtypeStruct((n, d), F32),
        scratch_shapes=[pltpu.VMEM((tm, f), BF16)],
        compiler_params=pltpu.CompilerParams(dimension_semantics=("arbitrary",),
                                             vmem_limit_bytes=VMEM_LIMIT),
        name="ffn_mix" if mix is not None else "ffn",
    )(*args)


def _proj_kernel(x_ref, pos_ref, pre_g_ref, w_ref, qg_ref, wq1_ref, wq2_ref, kvg_ref, wk_ref, wv_ref,
                 freq_ref, spread_ref, nonrope_ref, vones_ref, convw_ref, alog_ref, dtb_ref, ones_ref,
                 q_ref, k_ref, v_ref, gq_ref, gk_ref, gkb_ref, gvb_ref, g_ref, gate_ref,
                 xbuf_ref, *, tiles_per_seq):
    hn = _rms(x_ref[...], pre_g_ref[...]).astype(BF16)

    def blk(name):
        lo, hi = _PROJ_OFF[name]
        return _dot(hn, w_ref[:, lo:hi])

    w = GDN_W
    tm = x_ref.shape[0]
    reps = (1, N_MLA_HEADS)
    scale = (MLA_NOPE_DIM + MLA_ROPE_DIM) ** -0.5 * LOG2_E


    @pl.when(pl.program_id(0) % tiles_per_seq == 0)
    def _():
        xbuf_ref[0:CONV_HALO, :] = jnp.zeros((CONV_HALO, 3 * w), F32)

    def gdn_conv(part):
        cols = slice(part * w, (part + 1) * w)
        lo = _PROJ_OFF["qkv"][0] + part * w
        xbuf_ref[CONV_HALO:CONV_HALO + tm, cols] = _dot(hn, w_ref[:, lo:lo + w])
        xe = xbuf_ref[:, cols]
        y = xe[CONV_HALO:, :] * convw_ref[GDN_CONV - 1:GDN_CONV, cols]
        for back in range(1, GDN_CONV):
            shifted = pltpu.roll(xe, back, axis=0)[CONV_HALO:, :]
            y = y + shifted * convw_ref[GDN_CONV - 1 - back:GDN_CONV - back, cols]
        xbuf_ref[0:CONV_HALO, cols] = xe[tm:, :]
        return _silu_of_half(y)

    def rope_tables():
        ang = pos_ref[...].astype(F32) * freq_ref[...]
        tok = lax.broadcasted_iota(jnp.int32, (tm, LANES), 0) % POS_PACK
        own = lax.broadcasted_iota(jnp.int32, (tm, LANES), 1) // (MLA_ROPE_DIM // 2) == tok

        def to_rope_lanes(packed):
            per_tok = jnp.broadcast_to(packed[:, None, :], (tm // POS_PACK, POS_PACK, LANES)).reshape(tm, LANES)
            per_tok = jnp.where(own, per_tok, 0.0)
            hi = per_tok.astype(BF16)
            lo = (per_tok - hi.astype(F32)).astype(BF16)
            return _dot(hi, spread_ref[...]) + _dot(lo, spread_ref[...])

        return to_rope_lanes(jnp.cos(ang)) + nonrope_ref[...], to_rope_lanes(jnp.sin(ang))

    def mla_q(cos, sin):
        cq = _rms(blk("cq"), qg_ref[...]).astype(BF16)
        q = (_dot(cq, wq1_ref[...]) * jnp.tile(cos * scale, reps)
             + _dot(cq, wq2_ref[...]) * jnp.tile(sin * scale, reps))
        q_ref[...] = q.astype(BF16)

    def head_l2norm(x):
        return x * lax.rsqrt(_dot((x * x).astype(BF16), ones_ref[...]) + EPS)

    def gdn_q(gq):
        gq_ref[...] = (head_l2norm(gq) * (GDN_HEAD_DIM ** -0.5)).astype(BF16)

    def gdn_kv(gk, gv):
        gk = head_l2norm(gk)
        gk_ref[...] = gk.astype(BF16)
        beta = 0.5 * jnp.tanh(blk("b")) + 0.5
        gkb_ref[...] = (gk * beta).astype(BF16)
        gvb_ref[...] = (gv * beta).astype(BF16)

    def mla_kv(cos, sin):
        ckv = _rms(blk("ckv"), kvg_ref[...]).astype(BF16)
        kpe = blk("kpe") * cos + blk("kper") * sin
        k_ref[...] = (_dot(ckv, wk_ref[...]) + jnp.tile(kpe, reps)).astype(BF16)
        vt = (lax.dot_general(wv_ref[...], ckv, (((1,), (1,)), ((), ())), preferred_element_type=F32)
              + vones_ref[...]).astype(BF16)
        for kb in range(tm // ATTN_BK):
            v_ref[kb] = vt[:, kb * ATTN_BK:(kb + 1) * ATTN_BK]

    def gdn_gates():
        sp_in = blk("a") + dtb_ref[...]
        softplus = jnp.maximum(sp_in, 0.0) + jnp.log1p(jnp.exp(-jnp.abs(sp_in)))
        g_ref[...] = -jnp.exp(alog_ref[...]) * softplus
        gate_ref[...] = _silu_of_half(blk("gate"))

    gq = gdn_conv(0)
    cos, sin = rope_tables()
    gk = gdn_conv(1)
    mla_q(cos, sin)
    gdn_q(gq)
    gv = gdn_conv(2)
    mla_kv(cos, sin)
    gdn_kv(gk, gv)
    gdn_gates()


def _proj_call(x1, pos, consts, *, seq, tm=512):
    n, d = x1.shape
    row = lambda w: pl.BlockSpec((tm, w), lambda i: (i, 0))
    hw = N_MLA_HEADS * HEAD_PAD
    outs = [(hw, BF16), (hw, BF16),
            (GDN_W, BF16), (GDN_W, BF16), (GDN_W, BF16), (GDN_W, BF16), (GDN_W, F32), (GDN_W, F32)]
    vt_spec = pl.BlockSpec((tm // ATTN_BK, hw, ATTN_BK), lambda i: (i, 0, 0))
    vt_shape = jax.ShapeDtypeStruct((n // ATTN_BK, hw, ATTN_BK), BF16)
    return pl.pallas_call(
        functools.partial(_proj_kernel, tiles_per_seq=seq // tm),
        grid=(n // tm,),
        in_specs=([row(d), pl.BlockSpec((tm // POS_PACK, LANES), lambda i: (i, 0))]
                  + [_const_spec(c.shape) for c in consts]),
        out_specs=[row(w) for w, _ in outs[:2]] + [vt_spec] + [row(w) for w, _ in outs[2:]],
        out_shape=([jax.ShapeDtypeStruct((n, w), dt) for w, dt in outs[:2]] + [vt_shape]
                   + [jax.ShapeDtypeStruct((n, w), dt) for w, dt in outs[2:]]),
        scratch_shapes=[pltpu.VMEM((tm + CONV_HALO, 3 * GDN_W), F32)],
        compiler_params=pltpu.CompilerParams(dimension_semantics=("arbitrary",),
                                             vmem_limit_bytes=VMEM_LIMIT),
        name="proj",
    )(x1, pos, *consts)


def _attn_kernel(q_ref, k_ref, vt_ref, o_ref, m_ref, acc_ref, s_ref, *, seq, bq):
    bk = ATTN_BK
    nk = bq // bk
    heads = [slice(hh * HEAD_PAD, (hh + 1) * HEAD_PAD) for hh in range(2)]

    def kv_rows(j):
        return pl.ds(pl.multiple_of(j * bk, bk), bk)

    def scores(qs, j, buf, q_cols=slice(None)):
        for hh, sl in enumerate(heads):
            s_ref[buf, hh, :, q_cols] = lax.dot_general(k_ref[kv_rows(j), sl], qs[hh][q_cols],
                                                        (((1,), (1,)), ((), ())), preferred_element_type=F32)

    def accumulate(j, buf, q_cols=slice(None), mask=None):
        start = q_cols.start or 0
        stop = q_cols.stop or bq
        for hh, sl in enumerate(heads):
            for c0 in range(start, stop, ATTN_QSUB):
                cs = slice(c0, min(c0 + ATTN_QSUB, stop))
                s = s_ref[buf, hh, :, cs]
                if mask is not None:
                    s = jnp.where(mask[:, c0 - start:cs.stop - start], s, NEG_BIG)
                m_old = m_ref[hh, :, cs]
                m_new = jnp.maximum(m_old, jnp.max(s, axis=0, keepdims=True))
                p = jnp.exp2(s - m_new)
                acc_ref[hh, :, cs] = (jnp.exp2(m_old - m_new) * acc_ref[hh, :, cs]
                                      + _dot(vt_ref[j, sl, :], p.astype(BF16)))
                m_ref[hh, :, cs] = m_new

    def load_q(qi):
        return [q_ref[qi * bq:(qi + 1) * bq, sl] for sl in heads]

    nq = seq // bq
    qs = load_q(0)
    scores(qs, 0, 0)
    for qi in range(nq):
        m_ref[...] = jnp.full(m_ref.shape, NEG_BIG, F32)
        acc_ref[...] = jnp.zeros(acc_ref.shape, F32)

        def body(t, c, qs=qs):
            for d in range(nk):
                scores(qs, nk * t + d + 1, (d + 1) % 2)
                accumulate(nk * t + d, d % 2)
            return c

        lax.fori_loop(0, qi, body, 0)
        qs_next = load_q(qi + 1) if qi + 1 < nq else None
        for d in range(nk):
            if d + 1 < nk:
                scores(qs, nk * qi + d + 1, (d + 1) % 2, q_cols=slice((d + 1) * bk, bq))
            elif qs_next is not None:
                scores(qs_next, 0, (d + 1) % 2)
            accumulate(nk * qi + d, d % 2, q_cols=slice(d * bk, bq),
                       mask=(lax.broadcasted_iota(jnp.int32, (bk, bq - d * bk), 0)
                             <= lax.broadcasted_iota(jnp.int32, (bk, bq - d * bk), 1)))
        a0, a1 = acc_ref[0], acc_ref[1]
        num = jnp.concatenate([a0[:MLA_V_DIM], a1[MLA_V_DIM:]], axis=0)
        den = jnp.concatenate([a0[MLA_V_DIM:], a1[:MLA_V_DIM]], axis=0)
        o_ref[qi * bq:(qi + 1) * bq, :] = (num / den).T
        qs = qs_next


def _attn_call(q, k, vt, *, batch, seq, bq=1024):
    pairs = N_MLA_HEADS // 2
    in_spec = pl.BlockSpec((seq, 2 * HEAD_PAD), lambda b, p: (b, p))
    vt_spec = pl.BlockSpec((seq // ATTN_BK, 2 * HEAD_PAD, ATTN_BK), lambda b, p: (b, p, 0))
    return pl.pallas_call(
        functools.partial(_attn_kernel, seq=seq, bq=bq),
        grid=(batch, pairs),
        in_specs=[in_spec, in_spec, vt_spec],
        out_specs=pl.BlockSpec((seq, LANES), lambda b, p: (b, p)),
        out_shape=jax.ShapeDtypeStruct((batch * seq, MLA_W), F32),
        scratch_shapes=[pltpu.VMEM((2, 1, bq), F32), pltpu.VMEM((2, HEAD_PAD, bq), F32),
                        pltpu.VMEM((2, 2, ATTN_BK, bq), F32)],
        compiler_params=pltpu.CompilerParams(dimension_semantics=("arbitrary", "arbitrary"),
                                             vmem_limit_bytes=VMEM_LIMIT),
        name="attn",
    )(q, k, vt)


GROUP_HEADS = LANES // GDN_HEAD_DIM
GROUP_LANES = GROUP_HEADS * GDN_HEAD_DIM
GROUP_SLICES = tuple(slice(i * GROUP_LANES, (i + 1) * GROUP_LANES) for i in range(GDN_W // GROUP_LANES))


def _head_in_group(shape, dtype=jnp.int32):
    lane = lax.broadcasted_iota(jnp.int32, shape, 1) % GROUP_LANES // GDN_HEAD_DIM
    return lane.astype(F32).astype(dtype)


def _group_blockdiag(x_grp):
    x_grp = x_grp.astype(BF16)
    head = _head_in_group(x_grp.shape, BF16)
    zero = jnp.zeros_like(x_grp)
    return jnp.concatenate([jnp.where(head == h, x_grp, zero) for h in range(GROUP_HEADS)], axis=0)


def _heads_matmul(a_all, *b_alls):
    outs = [[] for _ in b_alls]
    for sl in GROUP_SLICES:
        rhs = jnp.concatenate([_group_blockdiag(b[:, sl]) for b in b_alls], axis=1)
        prod = _dot(a_all[:, sl].astype(BF16), rhs)
        for k, out in enumerate(outs):
            out.append(prod[:, k * GROUP_LANES:(k + 1) * GROUP_LANES])
    outs = [jnp.concatenate(out, axis=1) for out in outs]
    return outs[0] if len(outs) == 1 else outs


def _heads_gram(a_all, b_all):
    return jnp.concatenate([lax.dot_general(a_all[:, sl].astype(BF16), _group_blockdiag(b_all[:, sl]),
                                            (((1,), (1,)), ((), ())), preferred_element_type=F32)
                            for sl in GROUP_SLICES], axis=1)


def _heads_outer(a_all, *b_alls):
    outs = [[] for _ in b_alls]
    head = _head_in_group((GDN_HEAD_DIM, GROUP_LANES))
    for sl in GROUP_SLICES:
        rhs = jnp.concatenate([b[:, sl].astype(BF16) for b in b_alls], axis=1)
        full = lax.dot_general(a_all[:, sl].astype(BF16), rhs,
                               (((0,), (0,)), ((), ())), preferred_element_type=F32)
        for k, out in enumerate(outs):
            blk = full[:, k * GROUP_LANES:(k + 1) * GROUP_LANES]
            diag = blk[:GDN_HEAD_DIM]
            for h in range(1, GROUP_HEADS):
                diag = jnp.where(head == h, blk[h * GDN_HEAD_DIM:(h + 1) * GDN_HEAD_DIM], diag)
            out.append(diag)
    outs = [jnp.concatenate(out, axis=1) for out in outs]
    return outs[0] if len(outs) == 1 else outs


def _unit_lower_inverse(l_all, row, col):
    eye = jnp.where(row == col, 1.0, 0.0).astype(F32)
    zero = jnp.zeros_like(row, dtype=F32)
    same16 = (row // 16) == (col // 16)
    same32 = (row // 32) == (col // 32)
    mm = lambda xs, ys: [_heads_matmul(x, y) for x, y in zip(xs, ys)]
    m1 = [jnp.where(same16, -l, zero) for l in l_all]
    inv = [eye + m for m in m1]
    power = mm(m1, m1)
    for _ in range(2):
        both = [_heads_matmul(p, i, p) for p, i in zip(power, inv)]
        inv = [i + b[0] for i, b in zip(inv, both)]
        power = [b[1] for b in both]
    inv = [i + pi for i, pi in zip(inv, mm(power, inv))]
    for pick in (same32 & ~same16, ~same32):
        off = [jnp.where(pick, l, zero) for l in l_all]
        inv = [i - x for i, x in zip(inv, mm(inv, mm(off, inv)))]
    return inv


def _gdn_kernel(q_ref, k_ref, kb_ref, vb_ref, g_ref, gate_ref, ng_ref, ones_ref, o_ref, s_ref, *, tm):
    c = GDN_CHUNK
    w = GDN_W
    n_chunks = tm // c

    @pl.when(pl.program_id(1) == 0)
    def _():
        s_ref[...] = jnp.zeros(s_ref.shape, F32)

    row = lax.broadcasted_iota(jnp.int32, (c, w), 0)
    col = lax.broadcasted_iota(jnp.int32, (c, w), 1) % GDN_HEAD_DIM
    tril = jnp.where(lax.broadcasted_iota(jnp.int32, (c, c), 1) <= lax.broadcasted_iota(jnp.int32, (c, c), 0),
                     1.0, 0.0).astype(BF16)
    zero = jnp.zeros((c, w), F32)

    def tril_dot(x):
        hi = x.astype(BF16)
        lo = (x - hi.astype(F32)).astype(BF16)
        return _dot(tril, hi) + _dot(tril, lo)

    chunks = [(sq, slice(ci * c, (ci + 1) * c)) for sq in range(GDN_SEQS) for ci in range(n_chunks)]
    qs, ks, kbs = ([ref[sq, rs, :] for sq, rs in chunks] for ref in (q_ref, k_ref, kb_ref))
    gs = [g_ref[sq, rs, :] for sq, rs in chunks]
    sums = [tril_dot(jnp.concatenate([g, jnp.where(col < row, g, zero)], axis=1)) for g in gs]
    gcs = [sm[:, :w] for sm in sums]
    decays = [jnp.exp(jnp.where(col <= row, sm[:, w:], NEG_BIG)) for sm in sums]
    egcs = [jnp.exp(gc) for gc in gcs]
    grams = [_heads_gram(jnp.concatenate([kb, qc], axis=0), kc) for kb, qc, kc in zip(kbs, qs, ks)]
    t_alls = _unit_lower_inverse([jnp.where(col < row, gr[:c] * dc, zero) for gr, dc in zip(grams, decays)],
                                 row, col)
    attns = [gr[c:] * dc for gr, dc in zip(grams, decays)]
    wus = [_heads_matmul(t, kb.astype(F32) * e, vb_ref[sq, rs, :])
           for t, kb, e, (sq, rs) in zip(t_alls, kbs, egcs, chunks)]
    w_alls, u_alls = [wu[0] for wu in wus], [wu[1] for wu in wus]
    q_decs = [qc.astype(F32) * e for qc, e in zip(qs, egcs)]
    k_decs = [kc.astype(F32) * jnp.exp(gc[c - 1:c, :] - gc) for kc, gc in zip(ks, gcs)]
    s_decays = [jnp.exp(gc[c - 1:c, :]) for gc in gcs]

    attn_wu = [_heads_matmul(at, wa, ua) for at, wa, ua in zip(attns, w_alls, u_alls)]
    q_effs = [qd - awu[0] for qd, awu in zip(q_decs, attn_wu)]
    o_locals = [awu[1] for awu in attn_wu]
    k_wu = [_heads_outer(kd, wa, ua) for kd, wa, ua in zip(k_decs, w_alls, u_alls)]
    kws, kus = [x[0] for x in k_wu], [x[1] for x in k_wu]

    states = [s_ref[sq] for sq in range(GDN_SEQS)]
    outs = [[] for _ in range(GDN_SEQS)]
    for ci in range(n_chunks):
        for sq in range(GDN_SEQS):
            i = sq * n_chunks + ci
            prod = _heads_matmul(jnp.concatenate([q_effs[i], kws[i]], axis=0), states[sq])
            outs[sq].append(prod[:c] + o_locals[i])
            states[sq] = states[sq] * s_decays[i] - prod[c:] + kus[i]
    for sq in range(GDN_SEQS):
        s_ref[sq] = states[sq]
        o = jnp.concatenate(outs[sq], axis=0)
        var = _dot((o * o).astype(BF16), ones_ref[...]) * (1.0 / GDN_HEAD_DIM)
        o_ref[sq] = (o * lax.rsqrt(var + EPS) * ng_ref[...] * gate_ref[sq]).astype(BF16)


def _gdn_call(gq, gk, gkb, gvb, g_e, gate, ng_e, ones_bd, *, batch, seq, tm=256):
    nt = seq // tm
    n = batch * seq
    split = lambda a: a.reshape(GDN_SEQS, n // GDN_SEQS, GDN_W)
    row = pl.BlockSpec((GDN_SEQS, tm, GDN_W), lambda b, j: (0, b * nt + j, 0))
    consts = [ng_e, ones_bd]
    out = pl.pallas_call(
        functools.partial(_gdn_kernel, tm=tm),
        grid=(batch // GDN_SEQS, nt),
        in_specs=[row] * 6 + [_const_spec(x.shape) for x in consts],
        out_specs=row,
        out_shape=jax.ShapeDtypeStruct((GDN_SEQS, n // GDN_SEQS, GDN_W), BF16),
        scratch_shapes=[pltpu.VMEM((GDN_SEQS, GDN_HEAD_DIM, GDN_W), F32)],
        compiler_params=pltpu.CompilerParams(dimension_semantics=("arbitrary", "arbitrary"),
                                             vmem_limit_bytes=VMEM_LIMIT),
        name="gdn",
    )(*(split(a) for a in (gq, gk, gkb, gvb, g_e, gate)), *consts)
    return out.reshape(n, GDN_W)


def _rotate_half_cols(w):
    half = MLA_ROPE_DIM // 2
    return jnp.concatenate([-w[:, half:], w[:, :half]], axis=1)


def _prep_proj_weights(w_in, w_uq, w_ukv):
    d = w_in.shape[0]
    sizes = (MLA_Q_RANK, MLA_KV_RANK, MLA_ROPE_DIM, 3 * GDN_W, N_GDN_HEADS, N_GDN_HEADS, GDN_W)
    cuts = np.cumsum(sizes)[:-1]
    w_cq, w_ckv, w_kpe, w_qkv, w_a, w_b, w_gate = jnp.split(w_in, cuts, axis=1)

    def rope_block(wpe):
        z_lo = jnp.zeros((d, MLA_NOPE_DIM), F32)
        z_hi = jnp.zeros((d, HEAD_PAD - MLA_NOPE_DIM - MLA_ROPE_DIM), F32)
        return jnp.concatenate([z_lo, wpe, z_hi], axis=1)

    w_all = jnp.concatenate([
        w_cq, w_ckv, rope_block(w_kpe), rope_block(_rotate_half_cols(w_kpe)), w_qkv, 0.5 * w_gate,
        jnp.repeat(w_a, GDN_HEAD_DIM, axis=1), jnp.repeat(0.5 * w_b, GDN_HEAD_DIM, axis=1)],
        axis=1).astype(BF16)

    qh = w_uq.reshape(MLA_Q_RANK, N_MLA_HEADS, MLA_NOPE_DIM + MLA_ROPE_DIM)
    q_nope, q_pe = qh[..., :MLA_NOPE_DIM], qh[..., MLA_NOPE_DIM:]
    q_pe_rot = jnp.concatenate([-q_pe[..., MLA_ROPE_DIM // 2:], q_pe[..., :MLA_ROPE_DIM // 2]], axis=-1)
    pad = jnp.zeros((MLA_Q_RANK, N_MLA_HEADS, HEAD_PAD - MLA_NOPE_DIM - MLA_ROPE_DIM), F32)
    wq1 = jnp.concatenate([q_nope, q_pe, pad], axis=-1).reshape(MLA_Q_RANK, -1).astype(BF16)
    wq2 = jnp.concatenate([jnp.zeros_like(q_nope), q_pe_rot, pad], axis=-1).reshape(MLA_Q_RANK, -1).astype(BF16)

    kvh = w_ukv.reshape(MLA_KV_RANK, N_MLA_HEADS, MLA_NOPE_DIM + MLA_V_DIM)
    k_nope, v = kvh[..., :MLA_NOPE_DIM], kvh[..., MLA_NOPE_DIM:]
    kpad = jnp.zeros((MLA_KV_RANK, N_MLA_HEADS, HEAD_PAD - MLA_NOPE_DIM), F32)
    wk = jnp.concatenate([k_nope, kpad], axis=-1).reshape(MLA_KV_RANK, -1).astype(BF16)
    vp = v.reshape(MLA_KV_RANK, N_MLA_HEADS // 2, 2, MLA_V_DIM)
    vz = jnp.zeros_like(vp[:, :, 0])
    wv = jnp.concatenate([vp[:, :, 0], vz, vz, vp[:, :, 1]], axis=-1).reshape(MLA_KV_RANK, -1).T.astype(BF16)
    return w_all, wq1, wq2, wk, wv


def _value_ones_rows():
    pair = jnp.concatenate([jnp.zeros((MLA_V_DIM,), F32), jnp.ones((2 * MLA_V_DIM,), F32),
                            jnp.zeros((MLA_V_DIM,), F32)])
    return jnp.tile(pair, N_MLA_HEADS // 2)[:, None]


def _rope_constants():
    half = MLA_ROPE_DIM // 2
    freqs = ROPE_THETA ** (-jnp.arange(half, dtype=F32) / half)
    freq_packed = jnp.tile(freqs, LANES // half)[None, :]
    j = jnp.arange(LANES) % half
    lane = jnp.arange(LANES)
    spread = ((lane[None, :] == MLA_NOPE_DIM + j[:, None])
              | (lane[None, :] == MLA_NOPE_DIM + half + j[:, None])).astype(BF16)
    nonrope = ((lane < MLA_NOPE_DIM) | (lane >= MLA_NOPE_DIM + MLA_ROPE_DIM)).astype(F32)[None, :]
    return freq_packed, spread, nonrope


def kernel(x, positions, ffn1_pre_g, ffn1_w_gate, ffn1_w_up, ffn1_w_down, ffn1_post_g, mix_pre_g, w_in, mla_q_norm_g, mla_w_uq, mla_kv_norm_g, mla_w_ukv, mla_out_g, gdn_conv_w, gdn_a_log, gdn_dt_bias, gdn_norm_g, w_out, mix_post_g, ffn2_pre_g, ffn2_w_gate, ffn2_w_up, ffn2_w_down, ffn2_post_g):
    batch, seq, d = x.shape
    n = batch * seq
    xt = x.reshape(n, d)
    pos = jnp.repeat(positions.reshape(n), MLA_ROPE_DIM // 2).reshape(n // POS_PACK, LANES)
    freq, spread, nonrope = _rope_constants()
    head_ids = jnp.arange(GDN_W) // GDN_HEAD_DIM
    ones_bd = (head_ids[:, None] == head_ids[None, :]).astype(BF16)
    bf = lambda w: w.astype(BF16)
    r = lambda g: g[None, :]

    for l in range(ffn1_pre_g.shape[0]):
        xt = _ffn_call(xt, r(ffn1_pre_g[l]), bf(0.5 * ffn1_w_gate[l]), bf(ffn1_w_up[l]), bf(ffn1_w_down[l]),
                       r(ffn1_post_g[l]))
        w_all, wq1, wq2, wk, wv = _prep_proj_weights(w_in[l], mla_w_uq[l], mla_w_ukv[l])
        proj_consts = [r(mix_pre_g[l]), w_all, r(mla_q_norm_g[l]), wq1, wq2, r(mla_kv_norm_g[l]), wk, wv,
                       freq, spread, nonrope, _value_ones_rows(), 0.5 * gdn_conv_w[l],
                       r(jnp.repeat(gdn_a_log[l].astype(F32), GDN_HEAD_DIM)),
                       r(jnp.repeat(gdn_dt_bias[l].astype(F32), GDN_HEAD_DIM)), ones_bd]
        q, k, v, gq, gk, gkb, gvb, g_e, gate = _proj_call(xt, pos, proj_consts, seq=seq)
        mla_o = _attn_call(q, k, v, batch=batch, seq=seq)
        gdn_o = _gdn_call(gq, gk, gkb, gvb, g_e, gate, r(jnp.tile(gdn_norm_g[l], N_GDN_HEADS)), ones_bd,
                          batch=batch, seq=seq)
        xt = _ffn_call(xt, r(ffn2_pre_g[l]), bf(0.5 * ffn2_w_gate[l]), bf(ffn2_w_up[l]), bf(ffn2_w_down[l]),
                       r(ffn2_post_g[l]),
                       mix=(mla_o, gdn_o, r(mla_out_g[l]), bf(w_out[l]), r(mix_post_g[l])))
    return xt.reshape(batch, seq, d)
```

```python
import functools

import jax
import jax.numpy as jnp
import numpy as np
from jax import lax
from jax.experimental import pallas as pl
from jax.experimental.pallas import tpu as pltpu

F32 = jnp.float32
BF16 = jnp.bfloat16

N_MLA_HEADS = 8
MLA_Q_RANK = 256
MLA_KV_RANK = 128
MLA_NOPE_DIM = 64
MLA_ROPE_DIM = 32
MLA_V_DIM = 64
ROPE_THETA = 10000.0
N_GDN_HEADS = 8
GDN_HEAD_DIM = 64
GDN_CONV = 4
GDN_CHUNK = 64
MLA_W = N_MLA_HEADS * MLA_V_DIM
GDN_W = N_GDN_HEADS * GDN_HEAD_DIM
EPS = 1e-6

LANES = 128
POS_PACK = LANES // (MLA_ROPE_DIM // 2)

FFN_TM = 1024
FFN_SUBTILES = 2
FFN_FC = 256
PROJ_TM = 512
GDN_TM = 256
GDN_SEQS = 4
ATTN_BQ = 1024
ATTN_HEADS = 4
ATTN_BK = 256
ATTN_QSUB = 512
HEAD_PAD = 128
NEG_BIG = -1e30
LOG2_E = 1.4426950408889634
CONV_HALO = 8
VMEM_LIMIT = 56 * 1024 * 1024

_PROJ_BLOCKS = (("cq", MLA_Q_RANK), ("ckv", MLA_KV_RANK), ("kpe", LANES), ("kper", LANES),
                ("qkv", 3 * GDN_W), ("gate", GDN_W), ("a", GDN_W), ("b", GDN_W))
_PROJ_OFF = {}
_off = 0
for _name, _width in _PROJ_BLOCKS:
    _PROJ_OFF[_name] = (_off, _off + _width)
    _off += _width


def _rms(x, g):
    return x * lax.rsqrt(jnp.mean(x * x, axis=-1, keepdims=True) + EPS) * g


def _silu_of_half(z):
    return z + z * jnp.tanh(z)


def _dot(a, b):
    return jnp.dot(a, b, preferred_element_type=F32)


def _const_spec(shape):
    nd = len(shape)
    return pl.BlockSpec(shape, lambda *_: (0,) * nd, pipeline_mode=pl.Buffered(1))


def _swiglu_residual(x, pre_g, wg_ref, wu_ref, wd_ref, post_g, h_ref, fc):
    xn = _rms(x, pre_g).astype(BF16)
    for c in range(wg_ref.shape[1] // fc):
        sl = slice(c * fc, (c + 1) * fc)
        g = _dot(xn, wg_ref[:, sl])
        u = _dot(xn, wu_ref[:, sl])
        h_ref[:, sl] = (_silu_of_half(g) * u).astype(BF16)
    y = _dot(h_ref[...], wd_ref[...])
    return x + 0.5 * _rms(y, post_g)


def _ffn_kernel(x_ref, pre_g_ref, wg_ref, wu_ref, wd_ref, post_g_ref, o_ref, h_ref, *, fc):
    sub = x_ref.shape[0] // FFN_SUBTILES
    for r in range(FFN_SUBTILES):
        rows = slice(r * sub, (r + 1) * sub)
        o_ref[rows, :] = _swiglu_residual(x_ref[rows, :], pre_g_ref[...], wg_ref, wu_ref, wd_ref,
                                          post_g_ref[...], h_ref.at[rows, :], fc)


def _ffn_mix_kernel(x_ref, mla_ref, gdn_ref, mla_g_ref, wo_ref, mix_g_ref,
                    pre_g_ref, wg_ref, wu_ref, wd_ref, post_g_ref, o_ref, h_ref, *, fc):
    sub = x_ref.shape[0] // FFN_SUBTILES
    for r in range(FFN_SUBTILES):
        rows = slice(r * sub, (r + 1) * sub)
        mla = _rms(mla_ref[rows, :], mla_g_ref[...]).astype(BF16)
        mixed = _dot(mla, wo_ref[:MLA_W, :]) + _dot(gdn_ref[rows, :], wo_ref[MLA_W:, :])
        x2 = x_ref[rows, :] + _rms(mixed, mix_g_ref[...])
        o_ref[rows, :] = _swiglu_residual(x2, pre_g_ref[...], wg_ref, wu_ref, wd_ref,
                                          post_g_ref[...], h_ref.at[rows, :], fc)


def _ffn_call(x, pre_g, wg, wu, wd, post_g, mix=None, *, tm=FFN_TM, fc=FFN_FC):
    n, d = x.shape
    f = wg.shape[1]
    assert n % tm == 0 and tm % FFN_SUBTILES == 0 and f % fc == 0, (n, tm, f, fc)
    row = lambda w: pl.BlockSpec((tm, w), lambda i: (i, 0))
    ffn_specs = [_const_spec((1, d)), _const_spec((d, f)), _const_spec((d, f)),
                 _const_spec((f, d)), _const_spec((1, d))]
    ffn_args = [pre_g, wg, wu, wd, post_g]
    if mix is None:
        body, in_specs, args = _ffn_kernel, [row(d)] + ffn_specs, [x] + ffn_args
    else:
        mla_o, gdn_o, mla_g, wo, mix_g = mix
        body = _ffn_mix_kernel
        in_specs = [row(d), row(MLA_W), row(GDN_W), _const_spec((1, MLA_W)),
                    _const_spec(wo.shape), _const_spec((1, d))] + ffn_specs
        args = [x, mla_o, gdn_o, mla_g, wo, mix_g] + ffn_args
    return pl.pallas_call(
        functools.partial(body, fc=fc),
        grid=(n // tm,),
        in_specs=in_specs,
        out_specs=row(d),
        out_shape=jax.ShapeDtypeStruct((n, d), F32),
        scratch_shapes=[pltpu.VMEM((tm, f), BF16)],
        compiler_params=pltpu.CompilerParams(dimension_semantics=("arbitrary",),
                                             vmem_limit_bytes=VMEM_LIMIT),
        name="ffn_mix" if mix is not None else "ffn",
    )(*args)


def _proj_kernel(x_ref, pos_ref, pre_g_ref, w_ref, qg_ref, wq1_ref, wq2_ref, kvg_ref, wk_ref, wv_ref,
                 freq_ref, spread_ref, nonrope_ref, vones_ref, convw_ref, alog_ref, dtb_ref, ones_ref,
                 q_ref, k_ref, v_ref, gq_ref, gk_ref, gkb_ref, gvb_ref, g_ref, gate_ref,
                 xbuf_ref, *, tiles_per_seq):
    hn = _rms(x_ref[...], pre_g_ref[...]).astype(BF16)

    def blk(name):
        lo, hi = _PROJ_OFF[name]
        return _dot(hn, w_ref[:, lo:hi])

    w = GDN_W
    tm = x_ref.shape[0]
    reps = (1, N_MLA_HEADS)
    scale = (MLA_NOPE_DIM + MLA_ROPE_DIM) ** -0.5 * LOG2_E


    @pl.when(pl.program_id(0) % tiles_per_seq == 0)
    def _():
        xbuf_ref[0:CONV_HALO, :] = jnp.zeros((CONV_HALO, 3 * w), F32)

    def gdn_conv(part):
        cols = slice(part * w, (part + 1) * w)
        lo = _PROJ_OFF["qkv"][0] + part * w
        xbuf_ref[CONV_HALO:CONV_HALO + tm, cols] = _dot(hn, w_ref[:, lo:lo + w])
        xe = xbuf_ref[:, cols]
        y = xe[CONV_HALO:, :] * convw_ref[GDN_CONV - 1:GDN_CONV, cols]
        for back in range(1, GDN_CONV):
            shifted = pltpu.roll(xe, back, axis=0)[CONV_HALO:, :]
            y = y + shifted * convw_ref[GDN_CONV - 1 - back:GDN_CONV - back, cols]
        xbuf_ref[0:CONV_HALO, cols] = xe[tm:, :]
        return _silu_of_half(y)

    def rope_tables():
        ang = pos_ref[...].astype(F32) * freq_ref[...]
        tok = lax.broadcasted_iota(jnp.int32, (tm, LANES), 0) % POS_PACK
        own = lax.broadcasted_iota(jnp.int32, (tm, LANES), 1) // (MLA_ROPE_DIM // 2) == tok

        def to_rope_lanes(packed):
            per_tok = jnp.broadcast_to(packed[:, None, :], (tm // POS_PACK, POS_PACK, LANES)).reshape(tm, LANES)
            per_tok = jnp.where(own, per_tok, 0.0)
            hi = per_tok.astype(BF16)
            lo = (per_tok - hi.astype(F32)).astype(BF16)
            return _dot(hi, spread_ref[...]) + _dot(lo, spread_ref[...])

        return to_rope_lanes(jnp.cos(ang)) + nonrope_ref[...], to_rope_lanes(jnp.sin(ang))

    def mla_q(cos, sin):
        cq = _rms(blk("cq"), qg_ref[...]).astype(BF16)
        q = (_dot(cq, wq1_ref[...]) * jnp.tile(cos * scale, reps)
             + _dot(cq, wq2_ref[...]) * jnp.tile(sin * scale, reps))
        q_ref[...] = q.astype(BF16)

    def head_l2norm(x):
        return x * lax.rsqrt(_dot((x * x).astype(BF16), ones_ref[...]) + EPS)

    def gdn_q(gq):
        gq_ref[...] = (head_l2norm(gq) * (GDN_HEAD_DIM ** -0.5)).astype(BF16)

    def gdn_kv(gk, gv):
        gk = head_l2norm(gk)
        gk_ref[...] = gk.astype(BF16)
        beta = 0.5 * jnp.tanh(blk("b")) + 0.5
        gkb_ref[...] = (gk * beta).astype(BF16)
        gvb_ref[...] = (gv * beta).astype(BF16)

    def mla_kv(cos, sin):
        ckv = _rms(blk("ckv"), kvg_ref[...]).astype(BF16)
        kpe = blk("kpe") * cos + blk("kper") * sin
        k_ref[...] = (_dot(ckv, wk_ref[...]) + jnp.tile(kpe, reps)).astype(BF16)
        vt = (lax.dot_general(wv_ref[...], ckv, (((1,), (1,)), ((), ())), preferred_element_type=F32)
              + vones_ref[...]).astype(BF16)
        for kb in range(tm // ATTN_BK):
            v_ref[kb] = vt[:, kb * ATTN_BK:(kb + 1) * ATTN_BK]

    def gdn_gates():
        sp_in = blk("a") + dtb_ref[...]
        softplus = jnp.maximum(sp_in, 0.0) + jnp.log1p(jnp.exp(-jnp.abs(sp_in)))
        g_ref[...] = -jnp.exp(alog_ref[...]) * softplus
        gate_ref[...] = _silu_of_half(blk("gate"))

    gq = gdn_conv(0)
    cos, sin = rope_tables()
    gk = gdn_conv(1)
    mla_q(cos, sin)
    gdn_q(gq)
    gv = gdn_conv(2)
    mla_kv(cos, sin)
    gdn_kv(gk, gv)
    gdn_gates()


def _proj_call(x1, pos, consts, *, seq, tm=PROJ_TM):
    n, d = x1.shape
    assert seq % tm == 0 and tm % ATTN_BK == 0 and tm % POS_PACK == 0, (seq, tm)
    row = lambda w: pl.BlockSpec((tm, w), lambda i: (i, 0))
    hw = N_MLA_HEADS * HEAD_PAD
    outs = [(hw, BF16), (hw, BF16),
            (GDN_W, BF16), (GDN_W, BF16), (GDN_W, BF16), (GDN_W, BF16), (GDN_W, F32), (GDN_W, F32)]
    vt_spec = pl.BlockSpec((tm // ATTN_BK, hw, ATTN_BK), lambda i: (i, 0, 0))
    vt_shape = jax.ShapeDtypeStruct((n // ATTN_BK, hw, ATTN_BK), BF16)
    return pl.pallas_call(
        functools.partial(_proj_kernel, tiles_per_seq=seq // tm),
        grid=(n // tm,),
        in_specs=([row(d), pl.BlockSpec((tm // POS_PACK, LANES), lambda i: (i, 0))]
                  + [_const_spec(c.shape) for c in consts]),
        out_specs=[row(w) for w, _ in outs[:2]] + [vt_spec] + [row(w) for w, _ in outs[2:]],
        out_shape=([jax.ShapeDtypeStruct((n, w), dt) for w, dt in outs[:2]] + [vt_shape]
                   + [jax.ShapeDtypeStruct((n, w), dt) for w, dt in outs[2:]]),
        scratch_shapes=[pltpu.VMEM((tm + CONV_HALO, 3 * GDN_W), F32)],
        compiler_params=pltpu.CompilerParams(dimension_semantics=("arbitrary",),
                                             vmem_limit_bytes=VMEM_LIMIT),
        name="proj",
    )(x1, pos, *consts)


def _attn_kernel(q_ref, k_ref, vt_ref, o_ref, m_ref, acc_ref, s_ref, *, seq, bq):
    bk = ATTN_BK
    nk = bq // bk
    heads = [slice(hh * HEAD_PAD, (hh + 1) * HEAD_PAD) for hh in range(ATTN_HEADS)]

    def kv_rows(j):
        return pl.ds(pl.multiple_of(j * bk, bk), bk)

    def scores(qs, j, buf, q_cols=slice(None)):
        for hh, sl in enumerate(heads):
            s_ref[buf, hh, :, q_cols] = lax.dot_general(k_ref[kv_rows(j), sl], qs[hh][q_cols],
                                                        (((1,), (1,)), ((), ())), preferred_element_type=F32)

    def accumulate(j, buf, q_cols=slice(None), mask=None):
        start = q_cols.start or 0
        stop = q_cols.stop or bq
        for hh, sl in enumerate(heads):
            for c0 in range(start, stop, ATTN_QSUB):
                cs = slice(c0, min(c0 + ATTN_QSUB, stop))
                s = s_ref[buf, hh, :, cs]
                if mask is not None:
                    s = jnp.where(mask[:, c0 - start:cs.stop - start], s, NEG_BIG)
                m_old = m_ref[hh, :, cs]
                m_new = jnp.maximum(m_old, jnp.max(s, axis=0, keepdims=True))
                p = jnp.exp2(s - m_new)
                acc_ref[hh, :, cs] = (jnp.exp2(m_old - m_new) * acc_ref[hh, :, cs]
                                      + _dot(vt_ref[j, sl, :], p.astype(BF16)))
                m_ref[hh, :, cs] = m_new

    def load_q(qi):
        return [q_ref[qi * bq:(qi + 1) * bq, sl] for sl in heads]

    nq = seq // bq
    qs = load_q(0)
    scores(qs, 0, 0)
    for qi in range(nq):
        m_ref[...] = jnp.full(m_ref.shape, NEG_BIG, F32)
        acc_ref[...] = jnp.zeros(acc_ref.shape, F32)

        def body(t, c, qs=qs):
            for d in range(nk):
                scores(qs, nk * t + d + 1, (d + 1) % 2)
                accumulate(nk * t + d, d % 2)
            return c

        lax.fori_loop(0, qi, body, 0)
        qs_next = load_q(qi + 1) if qi + 1 < nq else None
        for d in range(nk):
            if d + 1 < nk:
                scores(qs, nk * qi + d + 1, (d + 1) % 2, q_cols=slice((d + 1) * bk, bq))
            elif qs_next is not None:
                scores(qs_next, 0, (d + 1) % 2)
            accumulate(nk * qi + d, d % 2, q_cols=slice(d * bk, bq),
                       mask=(lax.broadcasted_iota(jnp.int32, (bk, bq - d * bk), 0)
                             <= lax.broadcasted_iota(jnp.int32, (bk, bq - d * bk), 1)))
        for pair in range(ATTN_HEADS // 2):
            a0, a1 = acc_ref[2 * pair], acc_ref[2 * pair + 1]
            num = jnp.concatenate([a0[:MLA_V_DIM], a1[MLA_V_DIM:]], axis=0)
            den = jnp.concatenate([a0[MLA_V_DIM:], a1[:MLA_V_DIM]], axis=0)
            o_ref[qi * bq:(qi + 1) * bq, pair * LANES:(pair + 1) * LANES] = (num / den).T
        qs = qs_next


def _attn_call(q, k, vt, *, batch, seq, bq=ATTN_BQ):
    assert seq % bq == 0 and bq % (2 * ATTN_BK) == 0 and N_MLA_HEADS % ATTN_HEADS == 0, (seq, bq)
    groups = N_MLA_HEADS // ATTN_HEADS
    hw = ATTN_HEADS * HEAD_PAD
    in_spec = pl.BlockSpec((seq, hw), lambda b, p: (b, p))
    vt_spec = pl.BlockSpec((seq // ATTN_BK, hw, ATTN_BK), lambda b, p: (b, p, 0))
    return pl.pallas_call(
        functools.partial(_attn_kernel, seq=seq, bq=bq),
        grid=(batch, groups),
        in_specs=[in_spec, in_spec, vt_spec],
        out_specs=pl.BlockSpec((seq, ATTN_HEADS * MLA_V_DIM), lambda b, p: (b, p)),
        out_shape=jax.ShapeDtypeStruct((batch * seq, MLA_W), F32),
        scratch_shapes=[pltpu.VMEM((ATTN_HEADS, 1, bq), F32), pltpu.VMEM((ATTN_HEADS, HEAD_PAD, bq), F32),
                        pltpu.VMEM((2, ATTN_HEADS, ATTN_BK, bq), F32)],
        compiler_params=pltpu.CompilerParams(dimension_semantics=("arbitrary", "arbitrary"),
                                             vmem_limit_bytes=VMEM_LIMIT),
        name="attn",
    )(q, k, vt)


GROUP_HEADS = LANES // GDN_HEAD_DIM
GROUP_LANES = GROUP_HEADS * GDN_HEAD_DIM
GROUP_SLICES = tuple(slice(i * GROUP_LANES, (i + 1) * GROUP_LANES) for i in range(GDN_W // GROUP_LANES))


def _head_in_group(shape, dtype=jnp.int32):
    lane = lax.broadcasted_iota(jnp.int32, shape, 1) % GROUP_LANES // GDN_HEAD_DIM
    return lane.astype(F32).astype(dtype)


def _group_blockdiag(x_grp):
    x_grp = x_grp.astype(BF16)
    head = _head_in_group(x_grp.shape, BF16)
    zero = jnp.zeros_like(x_grp)
    return jnp.concatenate([jnp.where(head == h, x_grp, zero) for h in range(GROUP_HEADS)], axis=0)


def _heads_matmul(a_all, *b_alls):
    outs = [[] for _ in b_alls]
    for sl in GROUP_SLICES:
        rhs = jnp.concatenate([_group_blockdiag(b[:, sl]) for b in b_alls], axis=1)
        prod = _dot(a_all[:, sl].astype(BF16), rhs)
        for k, out in enumerate(outs):
            out.append(prod[:, k * GROUP_LANES:(k + 1) * GROUP_LANES])
    outs = [jnp.concatenate(out, axis=1) for out in outs]
    return outs[0] if len(outs) == 1 else outs


def _heads_gram(a_all, b_all):
    return jnp.concatenate([lax.dot_general(a_all[:, sl].astype(BF16), _group_blockdiag(b_all[:, sl]),
                                            (((1,), (1,)), ((), ())), preferred_element_type=F32)
                            for sl in GROUP_SLICES], axis=1)


def _heads_outer(a_all, *b_alls):
    outs = [[] for _ in b_alls]
    head = _head_in_group((GDN_HEAD_DIM, GROUP_LANES))
    for sl in GROUP_SLICES:
        rhs = jnp.concatenate([b[:, sl].astype(BF16) for b in b_alls], axis=1)
        full = lax.dot_general(a_all[:, sl].astype(BF16), rhs,
                               (((0,), (0,)), ((), ())), preferred_element_type=F32)
        for k, out in enumerate(outs):
            blk = full[:, k * GROUP_LANES:(k + 1) * GROUP_LANES]
            diag = blk[:GDN_HEAD_DIM]
            for h in range(1, GROUP_HEADS):
                diag = jnp.where(head == h, blk[h * GDN_HEAD_DIM:(h + 1) * GDN_HEAD_DIM], diag)
            out.append(diag)
    outs = [jnp.concatenate(out, axis=1) for out in outs]
    return outs[0] if len(outs) == 1 else outs


def _unit_lower_inverse(l_all, row, col):
    eye = jnp.where(row == col, 1.0, 0.0).astype(F32)
    zero = jnp.zeros_like(row, dtype=F32)
    same16 = (row // 16) == (col // 16)
    same32 = (row // 32) == (col // 32)
    mm = lambda xs, ys: [_heads_matmul(x, y) for x, y in zip(xs, ys)]
    m1 = [jnp.where(same16, -l, zero) for l in l_all]
    inv = [eye + m for m in m1]
    power = mm(m1, m1)
    for _ in range(2):
        both = [_heads_matmul(p, i, p) for p, i in zip(power, inv)]
        inv = [i + b[0] for i, b in zip(inv, both)]
        power = [b[1] for b in both]
    inv = [i + pi for i, pi in zip(inv, mm(power, inv))]
    for pick in (same32 & ~same16, ~same32):
        off = [jnp.where(pick, l, zero) for l in l_all]
        inv = [i - x for i, x in zip(inv, mm(inv, mm(off, inv)))]
    return inv


def _gdn_kernel(q_ref, k_ref, kb_ref, vb_ref, g_ref, gate_ref, ng_ref, ones_ref, o_ref, s_ref, *, tm):
    c = GDN_CHUNK
    w = GDN_W
    n_chunks = tm // c

    @pl.when(pl.program_id(1) == 0)
    def _():
        s_ref[...] = jnp.zeros(s_ref.shape, F32)

    row = lax.broadcasted_iota(jnp.int32, (c, w), 0)
    col = lax.broadcasted_iota(jnp.int32, (c, w), 1) % GDN_HEAD_DIM
    tril = jnp.where(lax.broadcasted_iota(jnp.int32, (c, c), 1) <= lax.broadcasted_iota(jnp.int32, (c, c), 0),
                     1.0, 0.0).astype(BF16)
    zero = jnp.zeros((c, w), F32)

    def tril_dot(x):
        hi = x.astype(BF16)
        lo = (x - hi.astype(F32)).astype(BF16)
        return _dot(tril, hi) + _dot(tril, lo)

    chunks = [(sq, slice(ci * c, (ci + 1) * c)) for sq in range(GDN_SEQS) for ci in range(n_chunks)]
    qs, ks, kbs = ([ref[sq, rs, :] for sq, rs in chunks] for ref in (q_ref, k_ref, kb_ref))
    gs = [g_ref[sq, rs, :] for sq, rs in chunks]
    sums = [tril_dot(jnp.concatenate([g, jnp.where(col < row, g, zero)], axis=1)) for g in gs]
    gcs = [sm[:, :w] for sm in sums]
    decays = [jnp.exp(jnp.where(col <= row, sm[:, w:], NEG_BIG)) for sm in sums]
    egcs = [jnp.exp(gc) for gc in gcs]
    grams = [_heads_gram(jnp.concatenate([kb, qc], axis=0), kc) for kb, qc, kc in zip(kbs, qs, ks)]
    t_alls = _unit_lower_inverse([jnp.where(col < row, gr[:c] * dc, zero) for gr, dc in zip(grams, decays)],
                                 row, col)
    attns = [gr[c:] * dc for gr, dc in zip(grams, decays)]
    wus = [_heads_matmul(t, kb.astype(F32) * e, vb_ref[sq, rs, :])
           for t, kb, e, (sq, rs) in zip(t_alls, kbs, egcs, chunks)]
    w_alls, u_alls = [wu[0] for wu in wus], [wu[1] for wu in wus]
    q_decs = [qc.astype(F32) * e for qc, e in zip(qs, egcs)]
    k_decs = [kc.astype(F32) * jnp.exp(gc[c - 1:c, :] - gc) for kc, gc in zip(ks, gcs)]
    s_decays = [jnp.exp(gc[c - 1:c, :]) for gc in gcs]

    attn_wu = [_heads_matmul(at, wa, ua) for at, wa, ua in zip(attns, w_alls, u_alls)]
    q_effs = [qd - awu[0] for qd, awu in zip(q_decs, attn_wu)]
    o_locals = [awu[1] for awu in attn_wu]
    k_wu = [_heads_outer(kd, wa, ua) for kd, wa, ua in zip(k_decs, w_alls, u_alls)]
    kws, kus = [x[0] for x in k_wu], [x[1] for x in k_wu]

    states = [s_ref[sq] for sq in range(GDN_SEQS)]
    outs = [[] for _ in range(GDN_SEQS)]
    for ci in range(n_chunks):
        for sq in range(GDN_SEQS):
            i = sq * n_chunks + ci
            prod = _heads_matmul(jnp.concatenate([q_effs[i], kws[i]], axis=0), states[sq])
            outs[sq].append(prod[:c] + o_locals[i])
            states[sq] = states[sq] * s_decays[i] - prod[c:] + kus[i]
    for sq in range(GDN_SEQS):
        s_ref[sq] = states[sq]
        o = jnp.concatenate(outs[sq], axis=0)
        var = _dot((o * o).astype(BF16), ones_ref[...]) * (1.0 / GDN_HEAD_DIM)
        o_ref[sq] = (o * lax.rsqrt(var + EPS) * ng_ref[...] * gate_ref[sq]).astype(BF16)


def _gdn_call(gq, gk, gkb, gvb, g_e, gate, ng_e, ones_bd, *, batch, seq, tm=GDN_TM):
    assert batch % GDN_SEQS == 0 and seq % tm == 0 and tm % GDN_CHUNK == 0, (batch, seq, tm)
    nt = seq // tm
    n = batch * seq
    split = lambda a: a.reshape(GDN_SEQS, n // GDN_SEQS, GDN_W)
    row = pl.BlockSpec((GDN_SEQS, tm, GDN_W), lambda b, j: (0, b * nt + j, 0))
    consts = [ng_e, ones_bd]
    out = pl.pallas_call(
        functools.partial(_gdn_kernel, tm=tm),
        grid=(batch // GDN_SEQS, nt),
        in_specs=[row] * 6 + [_const_spec(x.shape) for x in consts],
        out_specs=row,
        out_shape=jax.ShapeDtypeStruct((GDN_SEQS, n // GDN_SEQS, GDN_W), BF16),
        scratch_shapes=[pltpu.VMEM((GDN_SEQS, GDN_HEAD_DIM, GDN_W), F32)],
        compiler_params=pltpu.CompilerParams(dimension_semantics=("arbitrary", "arbitrary"),
                                             vmem_limit_bytes=VMEM_LIMIT),
        name="gdn",
    )(*(split(a) for a in (gq, gk, gkb, gvb, g_e, gate)), *consts)
    return out.reshape(n, GDN_W)


def _rotate_half_cols(w):
    half = MLA_ROPE_DIM // 2
    return jnp.concatenate([-w[:, half:], w[:, :half]], axis=1)


def _prep_proj_weights(w_in, w_uq, w_ukv):
    d = w_in.shape[0]
    sizes = (MLA_Q_RANK, MLA_KV_RANK, MLA_ROPE_DIM, 3 * GDN_W, N_GDN_HEADS, N_GDN_HEADS, GDN_W)
    cuts = np.cumsum(sizes)[:-1]
    w_cq, w_ckv, w_kpe, w_qkv, w_a, w_b, w_gate = jnp.split(w_in, cuts, axis=1)

    def rope_block(wpe):
        z_lo = jnp.zeros((d, MLA_NOPE_DIM), F32)
        z_hi = jnp.zeros((d, HEAD_PAD - MLA_NOPE_DIM - MLA_ROPE_DIM), F32)
        return jnp.concatenate([z_lo, wpe, z_hi], axis=1)

    w_all = jnp.concatenate([
        w_cq, w_ckv, rope_block(w_kpe), rope_block(_rotate_half_cols(w_kpe)), w_qkv, 0.5 * w_gate,
        jnp.repeat(w_a, GDN_HEAD_DIM, axis=1), jnp.repeat(0.5 * w_b, GDN_HEAD_DIM, axis=1)],
        axis=1).astype(BF16)

    qh = w_uq.reshape(MLA_Q_RANK, N_MLA_HEADS, MLA_NOPE_DIM + MLA_ROPE_DIM)
    q_nope, q_pe = qh[..., :MLA_NOPE_DIM], qh[..., MLA_NOPE_DIM:]
    q_pe_rot = jnp.concatenate([-q_pe[..., MLA_ROPE_DIM // 2:], q_pe[..., :MLA_ROPE_DIM // 2]], axis=-1)
    pad = jnp.zeros((MLA_Q_RANK, N_MLA_HEADS, HEAD_PAD - MLA_NOPE_DIM - MLA_ROPE_DIM), F32)
    wq1 = jnp.concatenate([q_nope, q_pe, pad], axis=-1).reshape(MLA_Q_RANK, -1).astype(BF16)
    wq2 = jnp.concatenate([jnp.zeros_like(q_nope), q_pe_rot, pad], axis=-1).reshape(MLA_Q_RANK, -1).astype(BF16)

    kvh = w_ukv.reshape(MLA_KV_RANK, N_MLA_HEADS, MLA_NOPE_DIM + MLA_V_DIM)
    k_nope, v = kvh[..., :MLA_NOPE_DIM], kvh[..., MLA_NOPE_DIM:]
    kpad = jnp.zeros((MLA_KV_RANK, N_MLA_HEADS, HEAD_PAD - MLA_NOPE_DIM), F32)
    wk = jnp.concatenate([k_nope, kpad], axis=-1).reshape(MLA_KV_RANK, -1).astype(BF16)
    vp = v.reshape(MLA_KV_RANK, N_MLA_HEADS // 2, 2, MLA_V_DIM)
    vz = jnp.zeros_like(vp[:, :, 0])
    wv = jnp.concatenate([vp[:, :, 0], vz, vz, vp[:, :, 1]], axis=-1).reshape(MLA_KV_RANK, -1).T.astype(BF16)
    return w_all, wq1, wq2, wk, wv


def _value_ones_rows():
    pair = jnp.concatenate([jnp.zeros((MLA_V_DIM,), F32), jnp.ones((2 * MLA_V_DIM,), F32),
                            jnp.zeros((MLA_V_DIM,), F32)])
    return jnp.tile(pair, N_MLA_HEADS // 2)[:, None]


def _rope_constants():
    half = MLA_ROPE_DIM // 2
    freqs = ROPE_THETA ** (-jnp.arange(half, dtype=F32) / half)
    freq_packed = jnp.tile(freqs, LANES // half)[None, :]
    j = jnp.arange(LANES) % half
    lane = jnp.arange(LANES)
    spread = ((lane[None, :] == MLA_NOPE_DIM + j[:, None])
              | (lane[None, :] == MLA_NOPE_DIM + half + j[:, None])).astype(BF16)
    nonrope = ((lane < MLA_NOPE_DIM) | (lane >= MLA_NOPE_DIM + MLA_ROPE_DIM)).astype(F32)[None, :]
    return freq_packed, spread, nonrope


def kernel(x, positions, ffn1_pre_g, ffn1_w_gate, ffn1_w_up, ffn1_w_down, ffn1_post_g, mix_pre_g, w_in, mla_q_norm_g, mla_w_uq, mla_kv_norm_g, mla_w_ukv, mla_out_g, gdn_conv_w, gdn_a_log, gdn_dt_bias, gdn_norm_g, w_out, mix_post_g, ffn2_pre_g, ffn2_w_gate, ffn2_w_up, ffn2_w_down, ffn2_post_g):
    batch, seq, d = x.shape
    n = batch * seq
    xt = x.reshape(n, d)
    pos = jnp.repeat(positions.reshape(n), MLA_ROPE_DIM // 2).reshape(n // POS_PACK, LANES)
    freq, spread, nonrope = _rope_constants()
    head_ids = jnp.arange(GDN_W) // GDN_HEAD_DIM
    ones_bd = (head_ids[:, None] == head_ids[None, :]).astype(BF16)
    bf = lambda w: w.astype(BF16)
    r = lambda g: g[None, :]

    for l in range(ffn1_pre_g.shape[0]):
        xt = _ffn_call(xt, r(ffn1_pre_g[l]), bf(0.5 * ffn1_w_gate[l]), bf(ffn1_w_up[l]), bf(ffn1_w_down[l]),
                       r(ffn1_post_g[l]))
        w_all, wq1, wq2, wk, wv = _prep_proj_weights(w_in[l], mla_w_uq[l], mla_w_ukv[l])
        proj_consts = [r(mix_pre_g[l]), w_all, r(mla_q_norm_g[l]), wq1, wq2, r(mla_kv_norm_g[l]), wk, wv,
                       freq, spread, nonrope, _value_ones_rows(), 0.5 * gdn_conv_w[l],
                       r(jnp.repeat(gdn_a_log[l].astype(F32), GDN_HEAD_DIM)),
                       r(jnp.repeat(gdn_dt_bias[l].astype(F32), GDN_HEAD_DIM)), ones_bd]
        q, k, v, gq, gk, gkb, gvb, g_e, gate = _proj_call(xt, pos, proj_consts, seq=seq)
        mla_o = _attn_call(q, k, v, batch=batch, seq=seq)
        gdn_o = _gdn_call(gq, gk, gkb, gvb, g_e, gate, r(jnp.tile(gdn_norm_g[l], N_GDN_HEADS)), ones_bd,
                          batch=batch, seq=seq)
        xt = _ffn_call(xt, r(ffn2_pre_g[l]), bf(0.5 * ffn2_w_gate[l]), bf(ffn2_w_up[l]), bf(ffn2_w_down[l]),
                       r(ffn2_post_g[l]),
                       mix=(mla_o, gdn_o, r(mla_out_g[l]), bf(w_out[l]), r(mix_post_g[l])))
    return xt.reshape(batch, seq, d)
```

```python
import functools

import jax
import jax.numpy as jnp
import numpy as np
from jax import lax
from jax.experimental import pallas as pl
from jax.experimental.pallas import tpu as pltpu

F32 = jnp.float32
BF16 = jnp.bfloat16

N_MLA_HEADS = 8
MLA_Q_RANK = 256
MLA_KV_RANK = 128
MLA_NOPE_DIM = 64
MLA_ROPE_DIM = 32
MLA_V_DIM = 64
ROPE_THETA = 10000.0
N_GDN_HEADS = 8
GDN_HEAD_DIM = 64
GDN_CONV = 4
GDN_CHUNK = 64
MLA_W = N_MLA_HEADS * MLA_V_DIM
GDN_W = N_GDN_HEADS * GDN_HEAD_DIM
EPS = 1e-6

LANES = 128
POS_PACK = LANES // (MLA_ROPE_DIM // 2)

FFN_TM = 1024
FFN_SUBTILES = 2
FFN_FC = 256
PROJ_TM = 512
GDN_TM = 256
GDN_SEQS = 4
ATTN_BQ = 1024
ATTN_HEADS = 4
ATTN_BK = 256
ATTN_QSUB = 512
HEAD_PAD = 128
NEG_BIG = -1e30
LOG2_E = 1.4426950408889634
CONV_HALO = 8
VMEM_LIMIT = 56 * 1024 * 1024

_PROJ_BLOCKS = (("cq", MLA_Q_RANK), ("ckv", MLA_KV_RANK), ("kpe", LANES), ("kper", LANES),
                ("qkv", 3 * GDN_W), ("gate", GDN_W), ("ab", LANES))
_PROJ_OFF = {}
_off = 0
for _name, _width in _PROJ_BLOCKS:
    _PROJ_OFF[_name] = (_off, _off + _width)
    _off += _width


def _rms(x, g):
    return x * lax.rsqrt(jnp.mean(x * x, axis=-1, keepdims=True) + EPS) * g


def _silu_of_half(z):
    return z + z * jnp.tanh(z)


def _dot(a, b):
    return jnp.dot(a, b, preferred_element_type=F32)


def _const_spec(shape):
    nd = len(shape)
    return pl.BlockSpec(shape, lambda *_: (0,) * nd, pipeline_mode=pl.Buffered(1))


def _swiglu_residual(x, pre_g, wg_ref, wu_ref, wd_ref, post_g, h_ref, fc):
    xn = _rms(x, pre_g).astype(BF16)
    for c in range(wg_ref.shape[1] // fc):
        sl = slice(c * fc, (c + 1) * fc)
        g = _dot(xn, wg_ref[:, sl])
        u = _dot(xn, wu_ref[:, sl])
        h_ref[:, sl] = (_silu_of_half(g) * u).astype(BF16)
    y = _dot(h_ref[...], wd_ref[...])
    return x + 0.5 * _rms(y, post_g)


def _ffn_kernel(x_ref, pre_g_ref, wg_ref, wu_ref, wd_ref, post_g_ref, o_ref, h_ref, *, fc):
    sub = x_ref.shape[0] // FFN_SUBTILES
    for r in range(FFN_SUBTILES):
        rows = slice(r * sub, (r + 1) * sub)
        o_ref[rows, :] = _swiglu_residual(x_ref[rows, :], pre_g_ref[...], wg_ref, wu_ref, wd_ref,
                                          post_g_ref[...], h_ref.at[rows, :], fc)


def _ffn_mix_kernel(x_ref, mla_ref, gdn_ref, mla_g_ref, wo_ref, mix_g_ref,
                    pre_g_ref, wg_ref, wu_ref, wd_ref, post_g_ref, o_ref, h_ref, *, fc):
    sub = x_ref.shape[0] // FFN_SUBTILES
    for r in range(FFN_SUBTILES):
        rows = slice(r * sub, (r + 1) * sub)
        mla = _rms(mla_ref[rows, :], mla_g_ref[...]).astype(BF16)
        mixed = _dot(mla, wo_ref[:MLA_W, :]) + _dot(gdn_ref[rows, :], wo_ref[MLA_W:, :])
        x2 = x_ref[rows, :] + _rms(mixed, mix_g_ref[...])
        o_ref[rows, :] = _swiglu_residual(x2, pre_g_ref[...], wg_ref, wu_ref, wd_ref,
                                          post_g_ref[...], h_ref.at[rows, :], fc)


def _ffn_call(x, pre_g, wg, wu, wd, post_g, mix=None, *, tm=FFN_TM, fc=FFN_FC):
    n, d = x.shape
    f = wg.shape[1]
    assert n % tm == 0 and tm % FFN_SUBTILES == 0 and f % fc == 0, (n, tm, f, fc)
    row = lambda w: pl.BlockSpec((tm, w), lambda i: (i, 0))
    ffn_specs = [_const_spec((1, d)), _const_spec((d, f)), _const_spec((d, f)),
                 _const_spec((f, d)), _const_spec((1, d))]
    ffn_args = [pre_g, wg, wu, wd, post_g]
    if mix is None:
        body, in_specs, args = _ffn_kernel, [row(d)] + ffn_specs, [x] + ffn_args
    else:
        mla_o, gdn_o, mla_g, wo, mix_g = mix
        body = _ffn_mix_kernel
        in_specs = [row(d), row(MLA_W), row(GDN_W), _const_spec((1, MLA_W)),
                    _const_spec(wo.shape), _const_spec((1, d))] + ffn_specs
        args = [x, mla_o, gdn_o, mla_g, wo, mix_g] + ffn_args
    return pl.pallas_call(
        functools.partial(body, fc=fc),
        grid=(n // tm,),
        in_specs=in_specs,
        out_specs=row(d),
        out_shape=jax.ShapeDtypeStruct((n, d), F32),
        scratch_shapes=[pltpu.VMEM((tm, f), BF16)],
        compiler_params=pltpu.CompilerParams(dimension_semantics=("arbitrary",),
                                             vmem_limit_bytes=VMEM_LIMIT),
        name="ffn_mix" if mix is not None else "ffn",
    )(*args)


def _proj_kernel(x_ref, pos_ref, pre_g_ref, w_ref, qg_ref, wq1_ref, wq2_ref, kvg_ref, wk_ref, wv_ref,
                 freq_ref, spread_ref, nonrope_ref, vones_ref, convw_ref, alog_ref, dtb_ref, exp_a_ref, exp_b_ref,
                 ones_ref,
                 q_ref, k_ref, v_ref, gq_ref, gk_ref, gkb_ref, gvb_ref, g_ref, gate_ref,
                 xbuf_ref, *, tiles_per_seq):
    hn = _rms(x_ref[...], pre_g_ref[...]).astype(BF16)

    def blk(name):
        lo, hi = _PROJ_OFF[name]
        return _dot(hn, w_ref[:, lo:hi])

    w = GDN_W
    tm = x_ref.shape[0]
    reps = (1, N_MLA_HEADS)
    scale = (MLA_NOPE_DIM + MLA_ROPE_DIM) ** -0.5 * LOG2_E


    @pl.when(pl.program_id(0) % tiles_per_seq == 0)
    def _():
        xbuf_ref[0:CONV_HALO, :] = jnp.zeros((CONV_HALO, 3 * w), F32)

    def gdn_conv(part):
        cols = slice(part * w, (part + 1) * w)
        lo = _PROJ_OFF["qkv"][0] + part * w
        xbuf_ref[CONV_HALO:CONV_HALO + tm, cols] = _dot(hn, w_ref[:, lo:lo + w])
        xe = xbuf_ref[:, cols]
        y = xe[CONV_HALO:, :] * convw_ref[GDN_CONV - 1:GDN_CONV, cols]
        for back in range(1, GDN_CONV):
            shifted = pltpu.roll(xe, back, axis=0)[CONV_HALO:, :]
            y = y + shifted * convw_ref[GDN_CONV - 1 - back:GDN_CONV - back, cols]
        xbuf_ref[0:CONV_HALO, cols] = xe[tm:, :]
        return _silu_of_half(y)

    def rope_tables():
        ang = pos_ref[...].astype(F32) * freq_ref[...]
        tok = lax.broadcasted_iota(jnp.int32, (tm, LANES), 0) % POS_PACK
        own = lax.broadcasted_iota(jnp.int32, (tm, LANES), 1) // (MLA_ROPE_DIM // 2) == tok

        def to_rope_lanes(packed):
            per_tok = jnp.broadcast_to(packed[:, None, :], (tm // POS_PACK, POS_PACK, LANES)).reshape(tm, LANES)
            per_tok = jnp.where(own, per_tok, 0.0)
            hi = per_tok.astype(BF16)
            lo = (per_tok - hi.astype(F32)).astype(BF16)
            return _dot(hi, spread_ref[...]) + _dot(lo, spread_ref[...])

        return to_rope_lanes(jnp.cos(ang)) + nonrope_ref[...], to_rope_lanes(jnp.sin(ang))

    def mla_q(cos, sin):
        cq = _rms(blk("cq"), qg_ref[...]).astype(BF16)
        q = (_dot(cq, wq1_ref[...]) * jnp.tile(cos * scale, reps)
             + _dot(cq, wq2_ref[...]) * jnp.tile(sin * scale, reps))
        q_ref[...] = q.astype(BF16)

    ab = blk("ab")

    def to_head_lanes(x, expand_ref):
        hi = x.astype(BF16)
        lo = (x - hi.astype(F32)).astype(BF16)
        return _dot(hi, expand_ref[...]) + _dot(lo, expand_ref[...])

    def head_l2norm(x):
        return x * lax.rsqrt(_dot((x * x).astype(BF16), ones_ref[...]) + EPS)

    def gdn_q(gq):
        gq_ref[...] = (head_l2norm(gq) * (GDN_HEAD_DIM ** -0.5)).astype(BF16)

    def gdn_kv(gk, gv):
        gk = head_l2norm(gk)
        gk_ref[...] = gk.astype(BF16)
        beta = to_head_lanes(0.5 * jnp.tanh(ab) + 0.5, exp_b_ref)
        gkb_ref[...] = (gk * beta).astype(BF16)
        gvb_ref[...] = (gv * beta).astype(BF16)

    def mla_kv(cos, sin):
        ckv = _rms(blk("ckv"), kvg_ref[...]).astype(BF16)
        kpe = blk("kpe") * cos + blk("kper") * sin
        k_ref[...] = (_dot(ckv, wk_ref[...]) + jnp.tile(kpe, reps)).astype(BF16)
        vt = (lax.dot_general(wv_ref[...], ckv, (((1,), (1,)), ((), ())), preferred_element_type=F32)
              + vones_ref[...]).astype(BF16)
        for kb in range(tm // ATTN_BK):
            v_ref[kb] = vt[:, kb * ATTN_BK:(kb + 1) * ATTN_BK]

    def gdn_gates():
        sp_in = ab + dtb_ref[...]
        softplus = jnp.maximum(sp_in, 0.0) + jnp.log1p(jnp.exp(-jnp.abs(sp_in)))
        g_ref[...] = to_head_lanes(-jnp.exp(alog_ref[...]) * softplus, exp_a_ref)
        gate_ref[...] = _silu_of_half(blk("gate"))

    gq = gdn_conv(0)
    cos, sin = rope_tables()
    gk = gdn_conv(1)
    mla_q(cos, sin)
    gdn_q(gq)
    gv = gdn_conv(2)
    mla_kv(cos, sin)
    gdn_kv(gk, gv)
    gdn_gates()


def _proj_call(x1, pos, consts, *, seq, tm=PROJ_TM):
    n, d = x1.shape
    assert seq % tm == 0 and tm % ATTN_BK == 0 and tm % POS_PACK == 0, (seq, tm)
    row = lambda w: pl.BlockSpec((tm, w), lambda i: (i, 0))
    hw = N_MLA_HEADS * HEAD_PAD
    outs = [(hw, BF16), (hw, BF16),
            (GDN_W, BF16), (GDN_W, BF16), (GDN_W, BF16), (GDN_W, BF16), (GDN_W, F32), (GDN_W, F32)]
    vt_spec = pl.BlockSpec((tm // ATTN_BK, hw, ATTN_BK), lambda i: (i, 0, 0))
    vt_shape = jax.ShapeDtypeStruct((n // ATTN_BK, hw, ATTN_BK), BF16)
    return pl.pallas_call(
        functools.partial(_proj_kernel, tiles_per_seq=seq // tm),
        grid=(n // tm,),
        in_specs=([row(d), pl.BlockSpec((tm // POS_PACK, LANES), lambda i: (i, 0))]
                  + [_const_spec(c.shape) for c in consts]),
        out_specs=[row(w) for w, _ in outs[:2]] + [vt_spec] + [row(w) for w, _ in outs[2:]],
        out_shape=([jax.ShapeDtypeStruct((n, w), dt) for w, dt in outs[:2]] + [vt_shape]
                   + [jax.ShapeDtypeStruct((n, w), dt) for w, dt in outs[2:]]),
        scratch_shapes=[pltpu.VMEM((tm + CONV_HALO, 3 * GDN_W), F32)],
        compiler_params=pltpu.CompilerParams(dimension_semantics=("arbitrary",),
                                             vmem_limit_bytes=VMEM_LIMIT),
        name="proj",
    )(x1, pos, *consts)


def _attn_kernel(q_ref, k_ref, vt_ref, o_ref, m_ref, acc_ref, s_ref, *, seq, bq):
    bk = ATTN_BK
    nk = bq // bk
    heads = [slice(hh * HEAD_PAD, (hh + 1) * HEAD_PAD) for hh in range(ATTN_HEADS)]

    def kv_rows(j):
        return pl.ds(pl.multiple_of(j * bk, bk), bk)

    def scores(qs, j, buf, q_cols=slice(None)):
        for hh, sl in enumerate(heads):
            s_ref[buf, hh, :, q_cols] = lax.dot_general(k_ref[kv_rows(j), sl], qs[hh][q_cols],
                                                        (((1,), (1,)), ((), ())), preferred_element_type=F32)

    def accumulate(j, buf, q_cols=slice(None), mask=None):
        start = q_cols.start or 0
        stop = q_cols.stop or bq
        for hh, sl in enumerate(heads):
            for c0 in range(start, stop, ATTN_QSUB):
                cs = slice(c0, min(c0 + ATTN_QSUB, stop))
                s = s_ref[buf, hh, :, cs]
                if mask is not None:
                    s = jnp.where(mask[:, c0 - start:cs.stop - start], s, NEG_BIG)
                m_old = m_ref[hh, :, cs]
                m_new = jnp.maximum(m_old, jnp.max(s, axis=0, keepdims=True))
                p = jnp.exp2(s - m_new)
                acc_ref[hh, :, cs] = (jnp.exp2(m_old - m_new) * acc_ref[hh, :, cs]
                                      + _dot(vt_ref[j, sl, :], p.astype(BF16)))
                m_ref[hh, :, cs] = m_new

    def load_q(qi):
        return [q_ref[qi * bq:(qi + 1) * bq, sl] for sl in heads]

    nq = seq // bq
    qs = load_q(0)
    scores(qs, 0, 0)
    for qi in range(nq):
        m_ref[...] = jnp.full(m_ref.shape, NEG_BIG, F32)
        acc_ref[...] = jnp.zeros(acc_ref.shape, F32)

        def body(t, c, qs=qs):
            for d in range(nk):
                scores(qs, nk * t + d + 1, (d + 1) % 2)
                accumulate(nk * t + d, d % 2)
            return c

        lax.fori_loop(0, qi, body, 0)
        qs_next = load_q(qi + 1) if qi + 1 < nq else None
        for d in range(nk):
            if d + 1 < nk:
                scores(qs, nk * qi + d + 1, (d + 1) % 2, q_cols=slice((d + 1) * bk, bq))
            elif qs_next is not None:
                scores(qs_next, 0, (d + 1) % 2)
            accumulate(nk * qi + d, d % 2, q_cols=slice(d * bk, bq),
                       mask=(lax.broadcasted_iota(jnp.int32, (bk, bq - d * bk), 0)
                             <= lax.broadcasted_iota(jnp.int32, (bk, bq - d * bk), 1)))
        for pair in range(ATTN_HEADS // 2):
            a0, a1 = acc_ref[2 * pair], acc_ref[2 * pair + 1]
            num = jnp.concatenate([a0[:MLA_V_DIM], a1[MLA_V_DIM:]], axis=0)
            den = jnp.concatenate([a0[MLA_V_DIM:], a1[:MLA_V_DIM]], axis=0)
            o_ref[qi * bq:(qi + 1) * bq, pair * LANES:(pair + 1) * LANES] = (num / den).T
        qs = qs_next


def _attn_call(q, k, vt, *, batch, seq, bq=ATTN_BQ):
    assert seq % bq == 0 and bq % (2 * ATTN_BK) == 0 and N_MLA_HEADS % ATTN_HEADS == 0, (seq, bq)
    groups = N_MLA_HEADS // ATTN_HEADS
    hw = ATTN_HEADS * HEAD_PAD
    in_spec = pl.BlockSpec((seq, hw), lambda b, p: (b, p))
    vt_spec = pl.BlockSpec((seq // ATTN_BK, hw, ATTN_BK), lambda b, p: (b, p, 0))
    return pl.pallas_call(
        functools.partial(_attn_kernel, seq=seq, bq=bq),
        grid=(batch, groups),
        in_specs=[in_spec, in_spec, vt_spec],
        out_specs=pl.BlockSpec((seq, ATTN_HEADS * MLA_V_DIM), lambda b, p: (b, p)),
        out_shape=jax.ShapeDtypeStruct((batch * seq, MLA_W), F32),
        scratch_shapes=[pltpu.VMEM((ATTN_HEADS, 1, bq), F32), pltpu.VMEM((ATTN_HEADS, HEAD_PAD, bq), F32),
                        pltpu.VMEM((2, ATTN_HEADS, ATTN_BK, bq), F32)],
        compiler_params=pltpu.CompilerParams(dimension_semantics=("arbitrary", "arbitrary"),
                                             vmem_limit_bytes=VMEM_LIMIT),
        name="attn",
    )(q, k, vt)


GROUP_HEADS = LANES // GDN_HEAD_DIM
GROUP_LANES = GROUP_HEADS * GDN_HEAD_DIM
GROUP_SLICES = tuple(slice(i * GROUP_LANES, (i + 1) * GROUP_LANES) for i in range(GDN_W // GROUP_LANES))


def _head_in_group(shape, dtype=jnp.int32):
    lane = lax.broadcasted_iota(jnp.int32, shape, 1) % GROUP_LANES // GDN_HEAD_DIM
    return lane.astype(F32).astype(dtype)


def _group_blockdiag(x_grp):
    x_grp = x_grp.astype(BF16)
    head = _head_in_group(x_grp.shape, BF16)
    zero = jnp.zeros_like(x_grp)
    return jnp.concatenate([jnp.where(head == h, x_grp, zero) for h in range(GROUP_HEADS)], axis=0)


def _heads_matmul(a_all, *b_alls):
    outs = [[] for _ in b_alls]
    for sl in GROUP_SLICES:
        rhs = jnp.concatenate([_group_blockdiag(b[:, sl]) for b in b_alls], axis=1)
        prod = _dot(a_all[:, sl].astype(BF16), rhs)
        for k, out in enumerate(outs):
            out.append(prod[:, k * GROUP_LANES:(k + 1) * GROUP_LANES])
    outs = [jnp.concatenate(out, axis=1) for out in outs]
    return outs[0] if len(outs) == 1 else outs


def _heads_gram(a_all, b_all):
    return jnp.concatenate([lax.dot_general(a_all[:, sl].astype(BF16), _group_blockdiag(b_all[:, sl]),
                                            (((1,), (1,)), ((), ())), preferred_element_type=F32)
                            for sl in GROUP_SLICES], axis=1)


def _heads_outer(a_all, *b_alls):
    outs = [[] for _ in b_alls]
    head = _head_in_group((GDN_HEAD_DIM, GROUP_LANES))
    for sl in GROUP_SLICES:
        rhs = jnp.concatenate([b[:, sl].astype(BF16) for b in b_alls], axis=1)
        full = lax.dot_general(a_all[:, sl].astype(BF16), rhs,
                               (((0,), (0,)), ((), ())), preferred_element_type=F32)
        for k, out in enumerate(outs):
            blk = full[:, k * GROUP_LANES:(k + 1) * GROUP_LANES]
            diag = blk[:GDN_HEAD_DIM]
            for h in range(1, GROUP_HEADS):
                diag = jnp.where(head == h, blk[h * GDN_HEAD_DIM:(h + 1) * GDN_HEAD_DIM], diag)
            out.append(diag)
    outs = [jnp.concatenate(out, axis=1) for out in outs]
    return outs[0] if len(outs) == 1 else outs


def _unit_lower_inverse(l_all, row, col):
    eye = jnp.where(row == col, 1.0, 0.0).astype(F32)
    zero = jnp.zeros_like(row, dtype=F32)
    same16 = (row // 16) == (col // 16)
    same32 = (row // 32) == (col // 32)
    mm = lambda xs, ys: [_heads_matmul(x, y) for x, y in zip(xs, ys)]
    m1 = [jnp.where(same16, -l, zero) for l in l_all]
    inv = [eye + m for m in m1]
    power = mm(m1, m1)
    for _ in range(2):
        both = [_heads_matmul(p, i, p) for p, i in zip(power, inv)]
        inv = [i + b[0] for i, b in zip(inv, both)]
        power = [b[1] for b in both]
    inv = [i + pi for i, pi in zip(inv, mm(power, inv))]
    for pick in (same32 & ~same16, ~same32):
        off = [jnp.where(pick, l, zero) for l in l_all]
        inv = [i - x for i, x in zip(inv, mm(inv, mm(off, inv)))]
    return inv


def _gdn_kernel(q_ref, k_ref, kb_ref, vb_ref, g_ref, gate_ref, ng_ref, ones_ref, o_ref, s_ref, *, tm):
    c = GDN_CHUNK
    w = GDN_W
    n_chunks = tm // c

    @pl.when(pl.program_id(1) == 0)
    def _():
        s_ref[...] = jnp.zeros(s_ref.shape, F32)

    row = lax.broadcasted_iota(jnp.int32, (c, w), 0)
    col = lax.broadcasted_iota(jnp.int32, (c, w), 1) % GDN_HEAD_DIM
    tril = jnp.where(lax.broadcasted_iota(jnp.int32, (c, c), 1) <= lax.broadcasted_iota(jnp.int32, (c, c), 0),
                     1.0, 0.0).astype(BF16)
    zero = jnp.zeros((c, w), F32)

    def tril_dot(x):
        hi = x.astype(BF16)
        lo = (x - hi.astype(F32)).astype(BF16)
        return _dot(tril, hi) + _dot(tril, lo)

    chunks = [(sq, slice(ci * c, (ci + 1) * c)) for sq in range(GDN_SEQS) for ci in range(n_chunks)]
    qs, ks, kbs = ([ref[sq, rs, :] for sq, rs in chunks] for ref in (q_ref, k_ref, kb_ref))
    gs = [g_ref[sq, rs, :] for sq, rs in chunks]
    sums = [tril_dot(jnp.concatenate([g, jnp.where(col < row, g, zero)], axis=1)) for g in gs]
    gcs = [sm[:, :w] for sm in sums]
    decays = [jnp.exp(jnp.where(col <= row, sm[:, w:], NEG_BIG)) for sm in sums]
    egcs = [jnp.exp(gc) for gc in gcs]
    grams = [_heads_gram(jnp.concatenate([kb, qc], axis=0), kc) for kb, qc, kc in zip(kbs, qs, ks)]
    t_alls = _unit_lower_inverse([jnp.where(col < row, gr[:c] * dc, zero) for gr, dc in zip(grams, decays)],
                                 row, col)
    attns = [gr[c:] * dc for gr, dc in zip(grams, decays)]
    wus = [_heads_matmul(t, kb.astype(F32) * e, vb_ref[sq, rs, :])
           for t, kb, e, (sq, rs) in zip(t_alls, kbs, egcs, chunks)]
    w_alls, u_alls = [wu[0] for wu in wus], [wu[1] for wu in wus]
    q_decs = [qc.astype(F32) * e for qc, e in zip(qs, egcs)]
    k_decs = [kc.astype(F32) * jnp.exp(gc[c - 1:c, :] - gc) for kc, gc in zip(ks, gcs)]
    s_decays = [jnp.exp(gc[c - 1:c, :]) for gc in gcs]

    attn_wu = [_heads_matmul(at, wa, ua) for at, wa, ua in zip(attns, w_alls, u_alls)]
    q_effs = [qd - awu[0] for qd, awu in zip(q_decs, attn_wu)]
    o_locals = [awu[1] for awu in attn_wu]
    k_wu = [_heads_outer(kd, wa, ua) for kd, wa, ua in zip(k_decs, w_alls, u_alls)]
    kws, kus = [x[0] for x in k_wu], [x[1] for x in k_wu]

    states = [s_ref[sq] for sq in range(GDN_SEQS)]
    outs = [[] for _ in range(GDN_SEQS)]
    for ci in range(n_chunks):
        for sq in range(GDN_SEQS):
            i = sq * n_chunks + ci
            prod = _heads_matmul(jnp.concatenate([q_effs[i], kws[i]], axis=0), states[sq])
            outs[sq].append(prod[:c] + o_locals[i])
            states[sq] = states[sq] * s_decays[i] - prod[c:] + kus[i]
    for sq in range(GDN_SEQS):
        s_ref[sq] = states[sq]
        o = jnp.concatenate(outs[sq], axis=0)
        var = _dot((o * o).astype(BF16), ones_ref[...]) * (1.0 / GDN_HEAD_DIM)
        o_ref[sq] = (o * lax.rsqrt(var + EPS) * ng_ref[...] * gate_ref[sq]).astype(BF16)


def _gdn_call(gq, gk, gkb, gvb, g_e, gate, ng_e, ones_bd, *, batch, seq, tm=GDN_TM):
    assert batch % GDN_SEQS == 0 and seq % tm == 0 and tm % GDN_CHUNK == 0, (batch, seq, tm)
    nt = seq // tm
    n = batch * seq
    split = lambda a: a.reshape(GDN_SEQS, n // GDN_SEQS, GDN_W)
    row = pl.BlockSpec((GDN_SEQS, tm, GDN_W), lambda b, j: (0, b * nt + j, 0))
    consts = [ng_e, ones_bd]
    out = pl.pallas_call(
        functools.partial(_gdn_kernel, tm=tm),
        grid=(batch // GDN_SEQS, nt),
        in_specs=[row] * 6 + [_const_spec(x.shape) for x in consts],
        out_specs=row,
        out_shape=jax.ShapeDtypeStruct((GDN_SEQS, n // GDN_SEQS, GDN_W), BF16),
        scratch_shapes=[pltpu.VMEM((GDN_SEQS, GDN_HEAD_DIM, GDN_W), F32)],
        compiler_params=pltpu.CompilerParams(dimension_semantics=("arbitrary", "arbitrary"),
                                             vmem_limit_bytes=VMEM_LIMIT),
        name="gdn",
    )(*(split(a) for a in (gq, gk, gkb, gvb, g_e, gate)), *consts)
    return out.reshape(n, GDN_W)


def _rotate_half_cols(w):
    half = MLA_ROPE_DIM // 2
    return jnp.concatenate([-w[:, half:], w[:, :half]], axis=1)


def _prep_proj_weights(w_in, w_uq, w_ukv):
    d = w_in.shape[0]
    sizes = (MLA_Q_RANK, MLA_KV_RANK, MLA_ROPE_DIM, 3 * GDN_W, N_GDN_HEADS, N_GDN_HEADS, GDN_W)
    cuts = np.cumsum(sizes)[:-1]
    w_cq, w_ckv, w_kpe, w_qkv, w_a, w_b, w_gate = jnp.split(w_in, cuts, axis=1)

    def rope_block(wpe):
        z_lo = jnp.zeros((d, MLA_NOPE_DIM), F32)
        z_hi = jnp.zeros((d, HEAD_PAD - MLA_NOPE_DIM - MLA_ROPE_DIM), F32)
        return jnp.concatenate([z_lo, wpe, z_hi], axis=1)

    w_all = jnp.concatenate([
        w_cq, w_ckv, rope_block(w_kpe), rope_block(_rotate_half_cols(w_kpe)), w_qkv, 0.5 * w_gate,
        w_a, 0.5 * w_b, jnp.zeros((d, LANES - 2 * N_GDN_HEADS), F32)],
        axis=1).astype(BF16)

    qh = w_uq.reshape(MLA_Q_RANK, N_MLA_HEADS, MLA_NOPE_DIM + MLA_ROPE_DIM)
    q_nope, q_pe = qh[..., :MLA_NOPE_DIM], qh[..., MLA_NOPE_DIM:]
    q_pe_rot = jnp.concatenate([-q_pe[..., MLA_ROPE_DIM // 2:], q_pe[..., :MLA_ROPE_DIM // 2]], axis=-1)
    pad = jnp.zeros((MLA_Q_RANK, N_MLA_HEADS, HEAD_PAD - MLA_NOPE_DIM - MLA_ROPE_DIM), F32)
    wq1 = jnp.concatenate([q_nope, q_pe, pad], axis=-1).reshape(MLA_Q_RANK, -1).astype(BF16)
    wq2 = jnp.concatenate([jnp.zeros_like(q_nope), q_pe_rot, pad], axis=-1).reshape(MLA_Q_RANK, -1).astype(BF16)

    kvh = w_ukv.reshape(MLA_KV_RANK, N_MLA_HEADS, MLA_NOPE_DIM + MLA_V_DIM)
    k_nope, v = kvh[..., :MLA_NOPE_DIM], kvh[..., MLA_NOPE_DIM:]
    kpad = jnp.zeros((MLA_KV_RANK, N_MLA_HEADS, HEAD_PAD - MLA_NOPE_DIM), F32)
    wk = jnp.concatenate([k_nope, kpad], axis=-1).reshape(MLA_KV_RANK, -1).astype(BF16)
    vp = v.reshape(MLA_KV_RANK, N_MLA_HEADS // 2, 2, MLA_V_DIM)
    vz = jnp.zeros_like(vp[:, :, 0])
    wv = jnp.concatenate([vp[:, :, 0], vz, vz, vp[:, :, 1]], axis=-1).reshape(MLA_KV_RANK, -1).T.astype(BF16)
    return w_all, wq1, wq2, wk, wv


def _head_expand(first_lane):
    src = jnp.arange(LANES)[:, None]
    head = (jnp.arange(GDN_W) // GDN_HEAD_DIM)[None, :]
    return (src == first_lane + head).astype(BF16)


def _value_ones_rows():
    pair = jnp.concatenate([jnp.zeros((MLA_V_DIM,), F32), jnp.ones((2 * MLA_V_DIM,), F32),
                            jnp.zeros((MLA_V_DIM,), F32)])
    return jnp.tile(pair, N_MLA_HEADS // 2)[:, None]


def _rope_constants():
    half = MLA_ROPE_DIM // 2
    freqs = ROPE_THETA ** (-jnp.arange(half, dtype=F32) / half)
    freq_packed = jnp.tile(freqs, LANES // half)[None, :]
    j = jnp.arange(LANES) % half
    lane = jnp.arange(LANES)
    spread = ((lane[None, :] == MLA_NOPE_DIM + j[:, None])
              | (lane[None, :] == MLA_NOPE_DIM + half + j[:, None])).astype(BF16)
    nonrope = ((lane < MLA_NOPE_DIM) | (lane >= MLA_NOPE_DIM + MLA_ROPE_DIM)).astype(F32)[None, :]
    return freq_packed, spread, nonrope


def kernel(x, positions, ffn1_pre_g, ffn1_w_gate, ffn1_w_up, ffn1_w_down, ffn1_post_g, mix_pre_g, w_in, mla_q_norm_g, mla_w_uq, mla_kv_norm_g, mla_w_ukv, mla_out_g, gdn_conv_w, gdn_a_log, gdn_dt_bias, gdn_norm_g, w_out, mix_post_g, ffn2_pre_g, ffn2_w_gate, ffn2_w_up, ffn2_w_down, ffn2_post_g):
    batch, seq, d = x.shape
    n = batch * seq
    xt = x.reshape(n, d)
    pos = jnp.repeat(positions.reshape(n), MLA_ROPE_DIM // 2).reshape(n // POS_PACK, LANES)
    freq, spread, nonrope = _rope_constants()
    head_ids = jnp.arange(GDN_W) // GDN_HEAD_DIM
    ones_bd = (head_ids[:, None] == head_ids[None, :]).astype(BF16)
    bf = lambda w: w.astype(BF16)
    r = lambda g: g[None, :]

    for l in range(ffn1_pre_g.shape[0]):
        xt = _ffn_call(xt, r(ffn1_pre_g[l]), bf(0.5 * ffn1_w_gate[l]), bf(ffn1_w_up[l]), bf(ffn1_w_down[l]),
                       r(ffn1_post_g[l]))
        w_all, wq1, wq2, wk, wv = _prep_proj_weights(w_in[l], mla_w_uq[l], mla_w_ukv[l])
        proj_consts = [r(mix_pre_g[l]), w_all, r(mla_q_norm_g[l]), wq1, wq2, r(mla_kv_norm_g[l]), wk, wv,
                       freq, spread, nonrope, _value_ones_rows(), 0.5 * gdn_conv_w[l],
                       r(jnp.pad(gdn_a_log[l].astype(F32), (0, LANES - N_GDN_HEADS))),
                       r(jnp.pad(gdn_dt_bias[l].astype(F32), (0, LANES - N_GDN_HEADS))),
                       _head_expand(0), _head_expand(N_GDN_HEADS), ones_bd]
        q, k, v, gq, gk, gkb, gvb, g_e, gate = _proj_call(xt, pos, proj_consts, seq=seq)
        mla_o = _attn_call(q, k, v, batch=batch, seq=seq)
        gdn_o = _gdn_call(gq, gk, gkb, gvb, g_e, gate, r(jnp.tile(gdn_norm_g[l], N_GDN_HEADS)), ones_bd,
                          batch=batch, seq=seq)
        xt = _ffn_call(xt, r(ffn2_pre_g[l]), bf(0.5 * ffn2_w_gate[l]), bf(ffn2_w_up[l]), bf(ffn2_w_down[l]),
                       r(ffn2_post_g[l]),
                       mix=(mla_o, gdn_o, r(mla_out_g[l]), bf(w_out[l]), r(mix_post_g[l])))
    return xt.reshape(batch, seq, d)
```

```python
import functools

import jax
import jax.numpy as jnp
import numpy as np
from jax import lax
from jax.experimental import pallas as pl
from jax.experimental.pallas import tpu as pltpu

F32 = jnp.float32
BF16 = jnp.bfloat16

N_MLA_HEADS = 8
MLA_Q_RANK = 256
MLA_KV_RANK = 128
MLA_NOPE_DIM = 64
MLA_ROPE_DIM = 32
MLA_V_DIM = 64
ROPE_THETA = 10000.0
N_GDN_HEADS = 8
GDN_HEAD_DIM = 64
GDN_CONV = 4
GDN_CHUNK = 64
MLA_W = N_MLA_HEADS * MLA_V_DIM
GDN_W = N_GDN_HEADS * GDN_HEAD_DIM
EPS = 1e-6

LANES = 128
POS_PACK = LANES // (MLA_ROPE_DIM // 2)

FFN_TM = 1024
FFN_SUBTILES = 2
FFN_FC = 256
PROJ_TM = 512
GDN_TM = 256
GDN_SEQS = 4
ATTN_BQ = 1024
ATTN_HEADS = 4
ATTN_BK = 256
ATTN_QSUB = 512
HEAD_PAD = 128
NEG_BIG = -1e30
LOG2_E = 1.4426950408889634
CONV_HALO = 8
VMEM_LIMIT = 56 * 1024 * 1024

_PROJ_BLOCKS = (("cq", MLA_Q_RANK), ("ckv", MLA_KV_RANK), ("kpe", LANES), ("kper", LANES),
                ("qkv", 3 * GDN_W), ("gate", GDN_W), ("ab", LANES))
_PROJ_OFF = {}
_off = 0
for _name, _width in _PROJ_BLOCKS:
    _PROJ_OFF[_name] = (_off, _off + _width)
    _off += _width


def _rms(x, g):
    return x * lax.rsqrt(jnp.mean(x * x, axis=-1, keepdims=True) + EPS) * g


def _silu_of_half(z):
    return z + z * jnp.tanh(z)


def _dot(a, b):
    return jnp.dot(a, b, preferred_element_type=F32)


def _const_spec(shape):
    nd = len(shape)
    return pl.BlockSpec(shape, lambda *_: (0,) * nd, pipeline_mode=pl.Buffered(1))


def _swiglu_residual(x, pre_g, wg_ref, wu_ref, wd_ref, post_g, h_ref, fc):
    xn = _rms(x, pre_g).astype(BF16)
    for c in range(wg_ref.shape[1] // fc):
        sl = slice(c * fc, (c + 1) * fc)
        g = _dot(xn, wg_ref[:, sl])
        u = _dot(xn, wu_ref[:, sl])
        h_ref[:, sl] = (_silu_of_half(g) * u).astype(BF16)
    y = _dot(h_ref[...], wd_ref[...])
    return x + 0.5 * _rms(y, post_g)


def _ffn_kernel(x_ref, pre_g_ref, wg_ref, wu_ref, wd_ref, post_g_ref, o_ref, h_ref, *, fc):
    sub = x_ref.shape[0] // FFN_SUBTILES
    for r in range(FFN_SUBTILES):
        rows = slice(r * sub, (r + 1) * sub)
        o_ref[rows, :] = _swiglu_residual(x_ref[rows, :], pre_g_ref[...], wg_ref, wu_ref, wd_ref,
                                          post_g_ref[...], h_ref.at[rows, :], fc)


def _ffn_mix_kernel(x_ref, mla_ref, gdn_ref, mla_g_ref, wo_ref, mix_g_ref,
                    pre_g_ref, wg_ref, wu_ref, wd_ref, post_g_ref, o_ref, h_ref, *, fc):
    sub = x_ref.shape[0] // FFN_SUBTILES
    for r in range(FFN_SUBTILES):
        rows = slice(r * sub, (r + 1) * sub)
        mla = _rms(mla_ref[rows, :], mla_g_ref[...]).astype(BF16)
        mixed = _dot(mla, wo_ref[:MLA_W, :]) + _dot(gdn_ref[rows, :], wo_ref[MLA_W:, :])
        x2 = x_ref[rows, :] + _rms(mixed, mix_g_ref[...])
        o_ref[rows, :] = _swiglu_residual(x2, pre_g_ref[...], wg_ref, wu_ref, wd_ref,
                                          post_g_ref[...], h_ref.at[rows, :], fc)


def _ffn_call(x, pre_g, wg, wu, wd, post_g, mix=None, *, tm=FFN_TM, fc=FFN_FC):
    n, d = x.shape
    f = wg.shape[1]
    assert n % tm == 0 and tm % FFN_SUBTILES == 0 and f % fc == 0, (n, tm, f, fc)
    row = lambda w: pl.BlockSpec((tm, w), lambda i: (i, 0))
    ffn_specs = [_const_spec((1, d)), _const_spec((d, f)), _const_spec((d, f)),
                 _const_spec((f, d)), _const_spec((1, d))]
    ffn_args = [pre_g, wg, wu, wd, post_g]
    if mix is None:
        body, in_specs, args = _ffn_kernel, [row(d)] + ffn_specs, [x] + ffn_args
    else:
        mla_o, gdn_o, mla_g, wo, mix_g = mix
        body = _ffn_mix_kernel
        in_specs = [row(d), row(MLA_W), row(GDN_W), _const_spec((1, MLA_W)),
                    _const_spec(wo.shape), _const_spec((1, d))] + ffn_specs
        args = [x, mla_o, gdn_o, mla_g, wo, mix_g] + ffn_args
    return pl.pallas_call(
        functools.partial(body, fc=fc),
        grid=(n // tm,),
        in_specs=in_specs,
        out_specs=row(d),
        out_shape=jax.ShapeDtypeStruct((n, d), F32),
        scratch_shapes=[pltpu.VMEM((tm, f), BF16)],
        compiler_params=pltpu.CompilerParams(dimension_semantics=("arbitrary",),
                                             vmem_limit_bytes=VMEM_LIMIT),
        name="ffn_mix" if mix is not None else "ffn",
    )(*args)


def _proj_kernel(x_ref, pos_ref, pre_g_ref, w_ref, qg_ref, wq1_ref, wq2_ref, kvg_ref, wk_ref, wv_ref,
                 freq_ref, spread_ref, nonrope_ref, vones_ref, convw_ref, alog_ref, dtb_ref, exp_a_ref, exp_b_ref,
                 ones_ref,
                 q_ref, k_ref, v_ref, gq_ref, gk_ref, gkb_ref, gvb_ref, g_ref, gate_ref,
                 xbuf_ref, *, tiles_per_seq):
    hn = _rms(x_ref[...], pre_g_ref[...]).astype(BF16)

    def blk(name):
        lo, hi = _PROJ_OFF[name]
        return _dot(hn, w_ref[:, lo:hi])

    w = GDN_W
    tm = x_ref.shape[0]
    reps = (1, N_MLA_HEADS)
    scale = (MLA_NOPE_DIM + MLA_ROPE_DIM) ** -0.5 * LOG2_E


    @pl.when(pl.program_id(0) % tiles_per_seq == 0)
    def _():
        xbuf_ref[0:CONV_HALO, :] = jnp.zeros((CONV_HALO, 3 * w), F32)

    def gdn_conv(part):
        cols = slice(part * w, (part + 1) * w)
        lo = _PROJ_OFF["qkv"][0] + part * w
        xbuf_ref[CONV_HALO:CONV_HALO + tm, cols] = _dot(hn, w_ref[:, lo:lo + w])
        xe = xbuf_ref[:, cols]
        y = xe[CONV_HALO:, :] * convw_ref[GDN_CONV - 1:GDN_CONV, cols]
        for back in range(1, GDN_CONV):
            shifted = pltpu.roll(xe, back, axis=0)[CONV_HALO:, :]
            y = y + shifted * convw_ref[GDN_CONV - 1 - back:GDN_CONV - back, cols]
        xbuf_ref[0:CONV_HALO, cols] = xe[tm:, :]
        return _silu_of_half(y)

    def rope_tables():
        ang = pos_ref[...].astype(F32) * freq_ref[...]
        tok = lax.broadcasted_iota(jnp.int32, (tm, LANES), 0) % POS_PACK
        own = lax.broadcasted_iota(jnp.int32, (tm, LANES), 1) // (MLA_ROPE_DIM // 2) == tok

        def to_rope_lanes(packed):
            per_tok = jnp.broadcast_to(packed[:, None, :], (tm // POS_PACK, POS_PACK, LANES)).reshape(tm, LANES)
            per_tok = jnp.where(own, per_tok, 0.0)
            hi = per_tok.astype(BF16)
            lo = (per_tok - hi.astype(F32)).astype(BF16)
            return _dot(hi, spread_ref[...]) + _dot(lo, spread_ref[...])

        return to_rope_lanes(jnp.cos(ang)) + nonrope_ref[...], to_rope_lanes(jnp.sin(ang))

    def mla_q(cos, sin):
        cq = _rms(blk("cq"), qg_ref[...]).astype(BF16)
        q = (_dot(cq, wq1_ref[...]) * jnp.tile(cos * scale, reps)
             + _dot(cq, wq2_ref[...]) * jnp.tile(sin * scale, reps))
        q_ref[...] = q.astype(BF16)

    ab = blk("ab")

    def to_head_lanes(x, expand_ref):
        hi = x.astype(BF16)
        lo = (x - hi.astype(F32)).astype(BF16)
        return _dot(hi, expand_ref[...]) + _dot(lo, expand_ref[...])

    def head_l2norm(x):
        return x * lax.rsqrt(_dot((x * x).astype(BF16), ones_ref[...]) + EPS)

    def gdn_q(gq):
        gq_ref[...] = (head_l2norm(gq) * (GDN_HEAD_DIM ** -0.5)).astype(BF16)

    def gdn_kv(gk, gv):
        gk = head_l2norm(gk)
        gk_ref[...] = gk.astype(BF16)
        beta = to_head_lanes(0.5 * jnp.tanh(ab) + 0.5, exp_b_ref)
        gkb_ref[...] = (gk * beta).astype(BF16)
        gvb_ref[...] = (gv * beta).astype(BF16)

    def mla_kv(cos, sin):
        ckv = _rms(blk("ckv"), kvg_ref[...]).astype(BF16)
        kpe = blk("kpe") * cos + blk("kper") * sin
        k_ref[...] = (_dot(ckv, wk_ref[...]) + jnp.tile(kpe, reps)).astype(BF16)
        vt = (lax.dot_general(wv_ref[...], ckv, (((1,), (1,)), ((), ())), preferred_element_type=F32)
              + vones_ref[...]).astype(BF16)
        for kb in range(tm // ATTN_BK):
            v_ref[kb] = vt[:, kb * ATTN_BK:(kb + 1) * ATTN_BK]

    def gdn_gates():
        sp_in = ab + dtb_ref[...]
        softplus = jnp.maximum(sp_in, 0.0) + jnp.log1p(jnp.exp(-jnp.abs(sp_in)))
        g_ref[...] = to_head_lanes(-jnp.exp(alog_ref[...]) * softplus, exp_a_ref)
        gate_ref[...] = _silu_of_half(blk("gate"))

    gq = gdn_conv(0)
    cos, sin = rope_tables()
    gdn_gates()
    gk = gdn_conv(1)
    mla_q(cos, sin)
    gdn_q(gq)
    gv = gdn_conv(2)
    mla_kv(cos, sin)
    gdn_kv(gk, gv)


def _proj_call(x1, pos, consts, *, seq, tm=PROJ_TM):
    n, d = x1.shape
    assert seq % tm == 0 and tm % ATTN_BK == 0 and tm % POS_PACK == 0, (seq, tm)
    row = lambda w: pl.BlockSpec((tm, w), lambda i: (i, 0))
    hw = N_MLA_HEADS * HEAD_PAD
    outs = [(hw, BF16), (hw, BF16),
            (GDN_W, BF16), (GDN_W, BF16), (GDN_W, BF16), (GDN_W, BF16), (GDN_W, F32), (GDN_W, F32)]
    vt_spec = pl.BlockSpec((tm // ATTN_BK, hw, ATTN_BK), lambda i: (i, 0, 0))
    vt_shape = jax.ShapeDtypeStruct((n // ATTN_BK, hw, ATTN_BK), BF16)
    return pl.pallas_call(
        functools.partial(_proj_kernel, tiles_per_seq=seq // tm),
        grid=(n // tm,),
        in_specs=([row(d), pl.BlockSpec((tm // POS_PACK, LANES), lambda i: (i, 0))]
                  + [_const_spec(c.shape) for c in consts]),
        out_specs=[row(w) for w, _ in outs[:2]] + [vt_spec] + [row(w) for w, _ in outs[2:]],
        out_shape=([jax.ShapeDtypeStruct((n, w), dt) for w, dt in outs[:2]] + [vt_shape]
                   + [jax.ShapeDtypeStruct((n, w), dt) for w, dt in outs[2:]]),
        scratch_shapes=[pltpu.VMEM((tm + CONV_HALO, 3 * GDN_W), F32)],
        compiler_params=pltpu.CompilerParams(dimension_semantics=("arbitrary",),
                                             vmem_limit_bytes=VMEM_LIMIT),
        name="proj",
    )(x1, pos, *consts)


def _attn_kernel(q_ref, k_ref, vt_ref, o_ref, m_ref, acc_ref, s_ref, *, seq, bq):
    bk = ATTN_BK
    nk = bq // bk
    heads = [slice(hh * HEAD_PAD, (hh + 1) * HEAD_PAD) for hh in range(ATTN_HEADS)]

    def kv_rows(j):
        return pl.ds(pl.multiple_of(j * bk, bk), bk)

    def scores(qs, j, buf, q_cols=slice(None)):
        for hh, sl in enumerate(heads):
            s_ref[buf, hh, :, q_cols] = lax.dot_general(k_ref[kv_rows(j), sl], qs[hh][q_cols],
                                                        (((1,), (1,)), ((), ())), preferred_element_type=F32)

    def accumulate(j, buf, q_cols=slice(None), mask=None):
        start = q_cols.start or 0
        stop = q_cols.stop or bq
        for hh, sl in enumerate(heads):
            for c0 in range(start, stop, ATTN_QSUB):
                cs = slice(c0, min(c0 + ATTN_QSUB, stop))
                s = s_ref[buf, hh, :, cs]
                if mask is not None:
                    s = jnp.where(mask[:, c0 - start:cs.stop - start], s, NEG_BIG)
                m_old = m_ref[hh, :, cs]
                m_new = jnp.maximum(m_old, jnp.max(s, axis=0, keepdims=True))
                p = jnp.exp2(s - m_new)
                acc_ref[hh, :, cs] = (jnp.exp2(m_old - m_new) * acc_ref[hh, :, cs]
                                      + _dot(vt_ref[j, sl, :], p.astype(BF16)))
                m_ref[hh, :, cs] = m_new

    def load_q(qi):
        return [q_ref[qi * bq:(qi + 1) * bq, sl] for sl in heads]

    nq = seq // bq
    qs = load_q(0)
    scores(qs, 0, 0)
    for qi in range(nq):
        m_ref[...] = jnp.full(m_ref.shape, NEG_BIG, F32)
        acc_ref[...] = jnp.zeros(acc_ref.shape, F32)

        def body(t, c, qs=qs):
            for d in range(nk):
                scores(qs, nk * t + d + 1, (d + 1) % 2)
                accumulate(nk * t + d, d % 2)
            return c

        lax.fori_loop(0, qi, body, 0)
        qs_next = load_q(qi + 1) if qi + 1 < nq else None
        for d in range(nk):
            if d + 1 < nk:
                scores(qs, nk * qi + d + 1, (d + 1) % 2, q_cols=slice((d + 1) * bk, bq))
            elif qs_next is not None:
                scores(qs_next, 0, (d + 1) % 2)
            accumulate(nk * qi + d, d % 2, q_cols=slice(d * bk, bq),
                       mask=(lax.broadcasted_iota(jnp.int32, (bk, bq - d * bk), 0)
                             <= lax.broadcasted_iota(jnp.int32, (bk, bq - d * bk), 1)))
        for pair in range(ATTN_HEADS // 2):
            a0, a1 = acc_ref[2 * pair], acc_ref[2 * pair + 1]
            num = jnp.concatenate([a0[:MLA_V_DIM], a1[MLA_V_DIM:]], axis=0)
            den = jnp.concatenate([a0[MLA_V_DIM:], a1[:MLA_V_DIM]], axis=0)
            o_ref[qi * bq:(qi + 1) * bq, pair * LANES:(pair + 1) * LANES] = (num / den).T
        qs = qs_next


def _attn_call(q, k, vt, *, batch, seq, bq=ATTN_BQ):
    assert seq % bq == 0 and bq % (2 * ATTN_BK) == 0 and N_MLA_HEADS % ATTN_HEADS == 0, (seq, bq)
    groups = N_MLA_HEADS // ATTN_HEADS
    hw = ATTN_HEADS * HEAD_PAD
    in_spec = pl.BlockSpec((seq, hw), lambda b, p: (b, p))
    vt_spec = pl.BlockSpec((seq // ATTN_BK, hw, ATTN_BK), lambda b, p: (b, p, 0))
    return pl.pallas_call(
        functools.partial(_attn_kernel, seq=seq, bq=bq),
        grid=(batch, groups),
        in_specs=[in_spec, in_spec, vt_spec],
        out_specs=pl.BlockSpec((seq, ATTN_HEADS * MLA_V_DIM), lambda b, p: (b, p)),
        out_shape=jax.ShapeDtypeStruct((batch * seq, MLA_W), F32),
        scratch_shapes=[pltpu.VMEM((ATTN_HEADS, 1, bq), F32), pltpu.VMEM((ATTN_HEADS, HEAD_PAD, bq), F32),
                        pltpu.VMEM((2, ATTN_HEADS, ATTN_BK, bq), F32)],
        compiler_params=pltpu.CompilerParams(dimension_semantics=("arbitrary", "arbitrary"),
                                             vmem_limit_bytes=VMEM_LIMIT),
        name="attn",
    )(q, k, vt)


GROUP_HEADS = LANES // GDN_HEAD_DIM
GROUP_LANES = GROUP_HEADS * GDN_HEAD_DIM
GROUP_SLICES = tuple(slice(i * GROUP_LANES, (i + 1) * GROUP_LANES) for i in range(GDN_W // GROUP_LANES))


def _head_in_group(shape, dtype=jnp.int32):
    lane = lax.broadcasted_iota(jnp.int32, shape, 1) % GROUP_LANES // GDN_HEAD_DIM
    return lane.astype(F32).astype(dtype)


def _group_blockdiag(x_grp):
    x_grp = x_grp.astype(BF16)
    head = _head_in_group(x_grp.shape, BF16)
    zero = jnp.zeros_like(x_grp)
    return jnp.concatenate([jnp.where(head == h, x_grp, zero) for h in range(GROUP_HEADS)], axis=0)


def _heads_matmul(a_all, *b_alls):
    outs = [[] for _ in b_alls]
    for sl in GROUP_SLICES:
        rhs = jnp.concatenate([_group_blockdiag(b[:, sl]) for b in b_alls], axis=1)
        prod = _dot(a_all[:, sl].astype(BF16), rhs)
        for k, out in enumerate(outs):
            out.append(prod[:, k * GROUP_LANES:(k + 1) * GROUP_LANES])
    outs = [jnp.concatenate(out, axis=1) for out in outs]
    return outs[0] if len(outs) == 1 else outs


def _heads_gram(a_all, b_all):
    return jnp.concatenate([lax.dot_general(a_all[:, sl].astype(BF16), _group_blockdiag(b_all[:, sl]),
                                            (((1,), (1,)), ((), ())), preferred_element_type=F32)
                            for sl in GROUP_SLICES], axis=1)


def _heads_outer(a_all, *b_alls):
    outs = [[] for _ in b_alls]
    head = _head_in_group((GDN_HEAD_DIM, GROUP_LANES))
    for sl in GROUP_SLICES:
        rhs = jnp.concatenate([b[:, sl].astype(BF16) for b in b_alls], axis=1)
        full = lax.dot_general(a_all[:, sl].astype(BF16), rhs,
                               (((0,), (0,)), ((), ())), preferred_element_type=F32)
        for k, out in enumerate(outs):
            blk = full[:, k * GROUP_LANES:(k + 1) * GROUP_LANES]
            diag = blk[:GDN_HEAD_DIM]
            for h in range(1, GROUP_HEADS):
                diag = jnp.where(head == h, blk[h * GDN_HEAD_DIM:(h + 1) * GDN_HEAD_DIM], diag)
            out.append(diag)
    outs = [jnp.concatenate(out, axis=1) for out in outs]
    return outs[0] if len(outs) == 1 else outs


def _unit_lower_inverse(l_all, row, col):
    eye = jnp.where(row == col, 1.0, 0.0).astype(F32)
    zero = jnp.zeros_like(row, dtype=F32)
    same16 = (row // 16) == (col // 16)
    same32 = (row // 32) == (col // 32)
    mm = lambda xs, ys: [_heads_matmul(x, y) for x, y in zip(xs, ys)]
    m1 = [jnp.where(same16, -l, zero) for l in l_all]
    inv = [eye + m for m in m1]
    power = mm(m1, m1)
    for _ in range(2):
        both = [_heads_matmul(p, i, p) for p, i in zip(power, inv)]
        inv = [i + b[0] for i, b in zip(inv, both)]
        power = [b[1] for b in both]
    inv = [i + pi for i, pi in zip(inv, mm(power, inv))]
    for pick in (same32 & ~same16, ~same32):
        off = [jnp.where(pick, l, zero) for l in l_all]
        inv = [i - x for i, x in zip(inv, mm(inv, mm(off, inv)))]
    return inv


def _gdn_kernel(q_ref, k_ref, kb_ref, vb_ref, g_ref, gate_ref, ng_ref, ones_ref, o_ref, s_ref, *, tm):
    c = GDN_CHUNK
    w = GDN_W
    n_chunks = tm // c

    @pl.when(pl.program_id(1) == 0)
    def _():
        s_ref[...] = jnp.zeros(s_ref.shape, F32)

    row = lax.broadcasted_iota(jnp.int32, (c, w), 0)
    col = lax.broadcasted_iota(jnp.int32, (c, w), 1) % GDN_HEAD_DIM
    tril = jnp.where(lax.broadcasted_iota(jnp.int32, (c, c), 1) <= lax.broadcasted_iota(jnp.int32, (c, c), 0),
                     1.0, 0.0).astype(BF16)
    zero = jnp.zeros((c, w), F32)

    def tril_dot(x):
        hi = x.astype(BF16)
        lo = (x - hi.astype(F32)).astype(BF16)
        return _dot(tril, hi) + _dot(tril, lo)

    chunks = [(sq, slice(ci * c, (ci + 1) * c)) for sq in range(GDN_SEQS) for ci in range(n_chunks)]
    qs, ks, kbs = ([ref[sq, rs, :] for sq, rs in chunks] for ref in (q_ref, k_ref, kb_ref))
    gs = [g_ref[sq, rs, :] for sq, rs in chunks]
    sums = [tril_dot(jnp.concatenate([g, jnp.where(col < row, g, zero)], axis=1)) for g in gs]
    gcs = [sm[:, :w] for sm in sums]
    decays = [jnp.exp(jnp.where(col <= row, sm[:, w:], NEG_BIG)) for sm in sums]
    egcs = [jnp.exp(gc) for gc in gcs]
    grams = [_heads_gram(jnp.concatenate([kb, qc], axis=0), kc) for kb, qc, kc in zip(kbs, qs, ks)]
    t_alls = _unit_lower_inverse([jnp.where(col < row, gr[:c] * dc, zero) for gr, dc in zip(grams, decays)],
                                 row, col)
    attns = [gr[c:] * dc for gr, dc in zip(grams, decays)]
    wus = [_heads_matmul(t, kb.astype(F32) * e, vb_ref[sq, rs, :])
           for t, kb, e, (sq, rs) in zip(t_alls, kbs, egcs, chunks)]
    w_alls, u_alls = [wu[0] for wu in wus], [wu[1] for wu in wus]
    q_decs = [qc.astype(F32) * e for qc, e in zip(qs, egcs)]
    k_decs = [kc.astype(F32) * jnp.exp(gc[c - 1:c, :] - gc) for kc, gc in zip(ks, gcs)]
    s_decays = [jnp.exp(gc[c - 1:c, :]) for gc in gcs]

    attn_wu = [_heads_matmul(at, wa, ua) for at, wa, ua in zip(attns, w_alls, u_alls)]
    q_effs = [qd - awu[0] for qd, awu in zip(q_decs, attn_wu)]
    o_locals = [awu[1] for awu in attn_wu]
    k_wu = [_heads_outer(kd, wa, ua) for kd, wa, ua in zip(k_decs, w_alls, u_alls)]
    kws, kus = [x[0] for x in k_wu], [x[1] for x in k_wu]

    states = [s_ref[sq] for sq in range(GDN_SEQS)]
    outs = [[] for _ in range(GDN_SEQS)]
    for ci in range(n_chunks):
        for sq in range(GDN_SEQS):
            i = sq * n_chunks + ci
            prod = _heads_matmul(jnp.concatenate([q_effs[i], kws[i]], axis=0), states[sq])
            outs[sq].append(prod[:c] + o_locals[i])
            states[sq] = states[sq] * s_decays[i] - prod[c:] + kus[i]
    for sq in range(GDN_SEQS):
        s_ref[sq] = states[sq]
        o = jnp.concatenate(outs[sq], axis=0)
        var = _dot((o * o).astype(BF16), ones_ref[...]) * (1.0 / GDN_HEAD_DIM)
        o_ref[sq] = (o * lax.rsqrt(var + EPS) * ng_ref[...] * gate_ref[sq]).astype(BF16)


def _gdn_call(gq, gk, gkb, gvb, g_e, gate, ng_e, ones_bd, *, batch, seq, tm=GDN_TM):
    assert batch % GDN_SEQS == 0 and seq % tm == 0 and tm % GDN_CHUNK == 0, (batch, seq, tm)
    nt = seq // tm
    n = batch * seq
    split = lambda a: a.reshape(GDN_SEQS, n // GDN_SEQS, GDN_W)
    row = pl.BlockSpec((GDN_SEQS, tm, GDN_W), lambda b, j: (0, b * nt + j, 0))
    consts = [ng_e, ones_bd]
    out = pl.pallas_call(
        functools.partial(_gdn_kernel, tm=tm),
        grid=(batch // GDN_SEQS, nt),
        in_specs=[row] * 6 + [_const_spec(x.shape) for x in consts],
        out_specs=row,
        out_shape=jax.ShapeDtypeStruct((GDN_SEQS, n // GDN_SEQS, GDN_W), BF16),
        scratch_shapes=[pltpu.VMEM((GDN_SEQS, GDN_HEAD_DIM, GDN_W), F32)],
        compiler_params=pltpu.CompilerParams(dimension_semantics=("arbitrary", "arbitrary"),
                                             vmem_limit_bytes=VMEM_LIMIT),
        name="gdn",
    )(*(split(a) for a in (gq, gk, gkb, gvb, g_e, gate)), *consts)
    return out.reshape(n, GDN_W)


def _rotate_half_cols(w):
    half = MLA_ROPE_DIM // 2
    return jnp.concatenate([-w[:, half:], w[:, :half]], axis=1)


def _prep_proj_weights(w_in, w_uq, w_ukv):
    d = w_in.shape[0]
    sizes = (MLA_Q_RANK, MLA_KV_RANK, MLA_ROPE_DIM, 3 * GDN_W, N_GDN_HEADS, N_GDN_HEADS, GDN_W)
    cuts = np.cumsum(sizes)[:-1]
    w_cq, w_ckv, w_kpe, w_qkv, w_a, w_b, w_gate = jnp.split(w_in, cuts, axis=1)

    def rope_block(wpe):
        z_lo = jnp.zeros((d, MLA_NOPE_DIM), F32)
        z_hi = jnp.zeros((d, HEAD_PAD - MLA_NOPE_DIM - MLA_ROPE_DIM), F32)
        return jnp.concatenate([z_lo, wpe, z_hi], axis=1)

    w_all = jnp.concatenate([
        w_cq, w_ckv, rope_block(w_kpe), rope_block(_rotate_half_cols(w_kpe)), w_qkv, 0.5 * w_gate,
        w_a, 0.5 * w_b, jnp.zeros((d, LANES - 2 * N_GDN_HEADS), F32)],
        axis=1).astype(BF16)

    qh = w_uq.reshape(MLA_Q_RANK, N_MLA_HEADS, MLA_NOPE_DIM + MLA_ROPE_DIM)
    q_nope, q_pe = qh[..., :MLA_NOPE_DIM], qh[..., MLA_NOPE_DIM:]
    q_pe_rot = jnp.concatenate([-q_pe[..., MLA_ROPE_DIM // 2:], q_pe[..., :MLA_ROPE_DIM // 2]], axis=-1)
    pad = jnp.zeros((MLA_Q_RANK, N_MLA_HEADS, HEAD_PAD - MLA_NOPE_DIM - MLA_ROPE_DIM), F32)
    wq1 = jnp.concatenate([q_nope, q_pe, pad], axis=-1).reshape(MLA_Q_RANK, -1).astype(BF16)
    wq2 = jnp.concatenate([jnp.zeros_like(q_nope), q_pe_rot, pad], axis=-1).reshape(MLA_Q_RANK, -1).astype(BF16)

    kvh = w_ukv.reshape(MLA_KV_RANK, N_MLA_HEADS, MLA_NOPE_DIM + MLA_V_DIM)
    k_nope, v = kvh[..., :MLA_NOPE_DIM], kvh[..., MLA_NOPE_DIM:]
    kpad = jnp.zeros((MLA_KV_RANK, N_MLA_HEADS, HEAD_PAD - MLA_NOPE_DIM), F32)
    wk = jnp.concatenate([k_nope, kpad], axis=-1).reshape(MLA_KV_RANK, -1).astype(BF16)
    vp = v.reshape(MLA_KV_RANK, N_MLA_HEADS // 2, 2, MLA_V_DIM)
    vz = jnp.zeros_like(vp[:, :, 0])
    wv = jnp.concatenate([vp[:, :, 0], vz, vz, vp[:, :, 1]], axis=-1).reshape(MLA_KV_RANK, -1).T.astype(BF16)
    return w_all, wq1, wq2, wk, wv


def _head_expand(first_lane):
    src = jnp.arange(LANES)[:, None]
    head = (jnp.arange(GDN_W) // GDN_HEAD_DIM)[None, :]
    return (src == first_lane + head).astype(BF16)


def _value_ones_rows():
    pair = jnp.concatenate([jnp.zeros((MLA_V_DIM,), F32), jnp.ones((2 * MLA_V_DIM,), F32),
                            jnp.zeros((MLA_V_DIM,), F32)])
    return jnp.tile(pair, N_MLA_HEADS // 2)[:, None]


def _rope_constants():
    half = MLA_ROPE_DIM // 2
    freqs = ROPE_THETA ** (-jnp.arange(half, dtype=F32) / half)
    freq_packed = jnp.tile(freqs, LANES // half)[None, :]
    j = jnp.arange(LANES) % half
    lane = jnp.arange(LANES)
    spread = ((lane[None, :] == MLA_NOPE_DIM + j[:, None])
              | (lane[None, :] == MLA_NOPE_DIM + half + j[:, None])).astype(BF16)
    nonrope = ((lane < MLA_NOPE_DIM) | (lane >= MLA_NOPE_DIM + MLA_ROPE_DIM)).astype(F32)[None, :]
    return freq_packed, spread, nonrope


def kernel(x, positions, ffn1_pre_g, ffn1_w_gate, ffn1_w_up, ffn1_w_down, ffn1_post_g, mix_pre_g, w_in, mla_q_norm_g, mla_w_uq, mla_kv_norm_g, mla_w_ukv, mla_out_g, gdn_conv_w, gdn_a_log, gdn_dt_bias, gdn_norm_g, w_out, mix_post_g, ffn2_pre_g, ffn2_w_gate, ffn2_w_up, ffn2_w_down, ffn2_post_g):
    batch, seq, d = x.shape
    n = batch * seq
    xt = x.reshape(n, d)
    pos = jnp.repeat(positions.reshape(n), MLA_ROPE_DIM // 2).reshape(n // POS_PACK, LANES)
    freq, spread, nonrope = _rope_constants()
    head_ids = jnp.arange(GDN_W) // GDN_HEAD_DIM
    ones_bd = (head_ids[:, None] == head_ids[None, :]).astype(BF16)
    bf = lambda w: w.astype(BF16)
    r = lambda g: g[None, :]

    for l in range(ffn1_pre_g.shape[0]):
        xt = _ffn_call(xt, r(ffn1_pre_g[l]), bf(0.5 * ffn1_w_gate[l]), bf(ffn1_w_up[l]), bf(ffn1_w_down[l]),
                       r(ffn1_post_g[l]))
        w_all, wq1, wq2, wk, wv = _prep_proj_weights(w_in[l], mla_w_uq[l], mla_w_ukv[l])
        proj_consts = [r(mix_pre_g[l]), w_all, r(mla_q_norm_g[l]), wq1, wq2, r(mla_kv_norm_g[l]), wk, wv,
                       freq, spread, nonrope, _value_ones_rows(), 0.5 * gdn_conv_w[l],
                       r(jnp.pad(gdn_a_log[l].astype(F32), (0, LANES - N_GDN_HEADS))),
                       r(jnp.pad(gdn_dt_bias[l].astype(F32), (0, LANES - N_GDN_HEADS))),
                       _head_expand(0), _head_expand(N_GDN_HEADS), ones_bd]
        q, k, v, gq, gk, gkb, gvb, g_e, gate = _proj_call(xt, pos, proj_consts, seq=seq)
        mla_o = _attn_call(q, k, v, batch=batch, seq=seq)
        gdn_o = _gdn_call(gq, gk, gkb, gvb, g_e, gate, r(jnp.tile(gdn_norm_g[l], N_GDN_HEADS)), ones_bd,
                          batch=batch, seq=seq)
        xt = _ffn_call(xt, r(ffn2_pre_g[l]), bf(0.5 * ffn2_w_gate[l]), bf(ffn2_w_up[l]), bf(ffn2_w_down[l]),
                       r(ffn2_post_g[l]),
                       mix=(mla_o, gdn_o, r(mla_out_g[l]), bf(w_out[l]), r(mix_post_g[l])))
    return xt.reshape(batch, seq, d)
```
